```python
import math
import jax, jax.numpy as jnp
from jax import lax
import numpy as np

D_MODEL = 1024
BATCH = 4
SEQ = 8192
DEPTH = 1
DEC_BATCH = 128
DEC_SEQ = 1
PAST_LEN = 16384
PAGE_SIZE = 128

HEADS_A = 16
KV_HEADS_A = 4
GROUP_A = HEADS_A // KV_HEADS_A
HEAD_DIM_A = D_MODEL // HEADS_A
WINDOW = 128
HEADS_B = 4
DK_B = D_MODEL // 2 // HEADS_B
DV_B = D_MODEL // HEADS_B
GATE_RANK = 16
GATE_TEMP = 16.0
GLA_CHUNK = 64
PEER_HEADS = 8
N_KEYS = 128
N_EXPERTS = N_KEYS * N_KEYS
PEER_TOPK = 16
PEER_DKEY = 256
PEER_DHALF = PEER_DKEY // 2
PEER_BLOCK = 256
EPS = 1e-6
COLS = (HEADS_A * HEAD_DIM_A, KV_HEADS_A * HEAD_DIM_A, KV_HEADS_A * HEAD_DIM_A,
        HEADS_B * DK_B, HEADS_B * DK_B, HEADS_B * DV_B, HEADS_B * DV_B,
        GATE_RANK, D_MODEL, D_MODEL)
IN_COLS = sum(COLS)

kernel_name = "hybrid_swa_gla_peer_step"


def rmsnorm(x, g):
    xf = x.astype(jnp.float32)
    y = xf * lax.rsqrt(jnp.mean(xf * xf, axis=-1, keepdims=True) + EPS) * g.astype(jnp.float32)
    return y.astype(x.dtype)


def alibi_slopes():
    return 2.0 ** (-8.0 * jnp.arange(1, HEADS_A + 1, dtype=jnp.float32) / HEADS_A)


def swa_sink_attend(q, k, v, q_pos, k_pos, sinks):
    f32 = jnp.float32
    s = jnp.einsum('btkgd,bskd->bkgts', q.astype(f32), k.astype(f32)) * (HEAD_DIM_A ** -0.5)
    dist = q_pos[:, None] - k_pos[None, :]
    valid = (dist >= 0) & (dist < WINDOW) & (k_pos[None, :] >= 0)
    slopes = alibi_slopes().reshape(KV_HEADS_A, GROUP_A)
    s = s - slopes[None, :, :, None, None] * dist.astype(f32)
    s = jnp.where(valid, s, -jnp.inf)
    sink = sinks.astype(f32).reshape(KV_HEADS_A, GROUP_A)[None, :, :, None, None]
    m = jnp.maximum(jnp.max(s, axis=-1, keepdims=True), sink)
    p = jnp.exp(s - m)
    denom = jnp.sum(p, axis=-1, keepdims=True) + jnp.exp(sink - m)
    o = jnp.einsum('bkgts,bskd->btkgd', p / denom, v.astype(f32))
    return o.astype(v.dtype)


def banded_swa(q, k, v, sinks):
    B, L = q.shape[:2]
    nb = L // WINDOW
    qb = q.reshape(B, nb, WINDOW, KV_HEADS_A, GROUP_A, HEAD_DIM_A).swapaxes(0, 1)

    def band(t):
        tp = jnp.pad(t, ((0, 0), (WINDOW, 0), (0, 0), (0, 0))).reshape(B, nb + 1, WINDOW, KV_HEADS_A, HEAD_DIM_A)
        return jnp.concatenate([tp[:, :-1], tp[:, 1:]], axis=2).swapaxes(0, 1)

    kb, vb = band(k), band(v)

    def one_block(args):
        j, qj, kj, vj = args
        q_pos = j * WINDOW + jnp.arange(WINDOW, dtype=jnp.int32)
        k_pos = (j - 1) * WINDOW + jnp.arange(2 * WINDOW, dtype=jnp.int32)
        return swa_sink_attend(qj, kj, vj, q_pos, k_pos, sinks)

    o = lax.map(one_block, (jnp.arange(nb, dtype=jnp.int32), qb, kb, vb))
    return o.swapaxes(0, 1).reshape(B, L, KV_HEADS_A, GROUP_A, HEAD_DIM_A)


def gla_recurrent(q, k, v, log_a, S0, chunk):
    f32 = jnp.float32
    B, L, H, DK = q.shape
    DV = v.shape[-1]
    n = L // chunk

    def to_chunks(t):
        return t.astype(f32).reshape(B, n, chunk, H, t.shape[-1]).transpose(1, 0, 3, 2, 4)

    causal = jnp.tril(jnp.ones((chunk, chunk), dtype=bool))

    def step(S, inp):
        qc, kc, vc, ac = inp
        b = jnp.cumsum(ac, axis=2)
        diff = b[:, :, :, None, :] - b[:, :, None, :, :]
        decay = jnp.exp(jnp.where(causal[:, :, None], diff, -jnp.inf))
        scores = jnp.einsum('bhtd,bhsd,bhtsd->bhts', qc, kc, decay)
        o = jnp.einsum('bhts,bhsv->bhtv', scores, vc) + jnp.einsum('bhtd,bhdv->bhtv', qc * jnp.exp(b), S)
        b_last = b[:, :, -1:, :]
        S = jnp.exp(b_last[:, :, 0, :])[..., None] * S + jnp.einsum('bhsd,bhsv->bhdv', kc * jnp.exp(b_last - b), vc)
        return S, o

    S, o = lax.scan(step, S0.astype(f32), (to_chunks(q), to_chunks(k), to_chunks(v), to_chunks(log_a)))
    o = o.transpose(1, 0, 3, 2, 4).reshape(B, L, H, DV)
    return o, S


def peer(h, peer_wq, peer_subkeys, peer_u, peer_v):
    f32 = jnp.float32
    D = h.shape[-1]
    xt = h.reshape(-1, D)
    T = xt.shape[0]
    blk = math.gcd(T, PEER_BLOCK)
    xb = xt.reshape(T // blk, blk, D)

    def one_block(xk):
        q = (xk @ peer_wq).astype(f32).reshape(blk, PEER_HEADS, 2, PEER_DHALF)
        s = jnp.einsum('thcd,hcnd->thcn', q, peer_subkeys.astype(f32))
        val, idx = lax.top_k(s, PEER_TOPK)
        cand = (val[:, :, 0, :, None] + val[:, :, 1, None, :]).reshape(blk, PEER_HEADS, PEER_TOPK * PEER_TOPK)
        cidx = (idx[:, :, 0, :, None] * N_KEYS + idx[:, :, 1, None, :]).reshape(blk, PEER_HEADS, PEER_TOPK * PEER_TOPK)
        sv, si = lax.top_k(cand, PEER_TOPK)
        eidx = jnp.take_along_axis(cidx, si, axis=-1)
        g = jax.nn.softmax(sv, axis=-1)
        ue = jnp.take(peer_u, eidx, axis=0)
        act = jax.nn.gelu(jnp.einsum('thkd,td->thk', ue.astype(f32), xk.astype(f32)), approximate=False)
        ve = jnp.take(peer_v, eidx, axis=0)
        return jnp.einsum('thk,thkd->td', g * act, ve.astype(f32)).astype(h.dtype)

    return lax.map(one_block, xb).reshape(h.shape)


def layer(x, prompt_mode, k_prev, v_prev, S0, norm1_g, w_in, w_gla_gate, b_gla_gate, attn_sinks,
          gla_norm_g, w_out, norm2_g, peer_wq, peer_subkeys, peer_u, peer_v):
    f32 = jnp.float32
    B, L, D = x.shape
    n1 = rmsnorm(x, norm1_g)
    proj = n1 @ w_in
    split_idx = np.cumsum(COLS)[:-1].tolist()
    qa, ka, va, qb, kb, vb, rb, zb, ga, gb = jnp.split(proj, split_idx, axis=-1)
    qa = qa.reshape(B, L, KV_HEADS_A, GROUP_A, HEAD_DIM_A)
    ka = ka.reshape(B, L, KV_HEADS_A, HEAD_DIM_A)
    va = va.reshape(B, L, KV_HEADS_A, HEAD_DIM_A)
    if prompt_mode:
        oa = banded_swa(qa, ka, va, attn_sinks)
        k_all, v_all = ka, va
    else:
        k_all = jnp.concatenate([k_prev.astype(ka.dtype), ka], axis=1)
        v_all = jnp.concatenate([v_prev.astype(va.dtype), va], axis=1)
        q_pos = PAST_LEN + jnp.arange(L, dtype=jnp.int32)
        k_pos = PAST_LEN - WINDOW + jnp.arange(WINDOW + L, dtype=jnp.int32)
        oa = swa_sink_attend(qa, k_all, v_all, q_pos, k_pos, attn_sinks)
    new_k = k_all[:, -WINDOW:]
    new_v = v_all[:, -WINDOW:]
    oa = oa.reshape(B, L, D)
    log_a = jax.nn.log_sigmoid((zb @ w_gla_gate + b_gla_gate).astype(f32)) / GATE_TEMP
    log_a = log_a.reshape(B, L, HEADS_B, DK_B)
    qg = qb.reshape(B, L, HEADS_B, DK_B).astype(f32) * (DK_B ** -0.5)
    kg = kb.reshape(B, L, HEADS_B, DK_B)
    vg = vb.reshape(B, L, HEADS_B, DV_B)
    og, S_new = gla_recurrent(qg, kg, vg, log_a, S0, math.gcd(L, GLA_CHUNK))
    og = rmsnorm(og, gla_norm_g) * jax.nn.silu(rb.reshape(B, L, HEADS_B, DV_B).astype(f32))
    og = og.reshape(B, L, D).astype(x.dtype)
    u = jax.nn.sigmoid(ga) * oa + jax.nn.sigmoid(gb) * og
    x = x + u @ w_out
    x = x + peer(rmsnorm(x, norm2_g), peer_wq, peer_subkeys, peer_u, peer_v)
    return x, new_k, new_v, S_new


def setup_inputs(seed: int = 0) -> dict:
    key = jax.random.key(seed)
    ks = jax.random.split(key, 20)
    f32 = jnp.float32
    nrm = lambda k, shape, scale: jax.random.normal(k, shape, f32) * scale
    return {
        "x_prompt": nrm(ks[0], (BATCH, SEQ, D_MODEL), 1.0),
        "x_sample": nrm(ks[1], (DEC_BATCH, DEC_SEQ, D_MODEL), 1.0),
        "cache_k_win": nrm(ks[2], (DEPTH, DEC_BATCH, WINDOW, KV_HEADS_A, HEAD_DIM_A), 1.0),
        "cache_v_win": nrm(ks[3], (DEPTH, DEC_BATCH, WINDOW, KV_HEADS_A, HEAD_DIM_A), 1.0),
        "state_gla": nrm(ks[4], (DEPTH, DEC_BATCH, HEADS_B, DK_B, DV_B), 1.0),
        "norm1_g": 1.0 + nrm(ks[5], (DEPTH, D_MODEL), 0.05),
        "w_in": nrm(ks[6], (DEPTH, D_MODEL, IN_COLS), D_MODEL ** -0.5),
        "w_gla_gate": nrm(ks[7], (DEPTH, GATE_RANK, HEADS_B * DK_B), GATE_RANK ** -0.5),
        "b_gla_gate": nrm(ks[8], (DEPTH, HEADS_B * DK_B), 0.1),
        "attn_sinks": nrm(ks[9], (DEPTH, HEADS_A), 0.5),
        "gla_norm_g": 1.0 + nrm(ks[10], (DEPTH, DV_B), 0.05),
        "w_out": nrm(ks[11], (DEPTH, D_MODEL, D_MODEL), D_MODEL ** -0.5),
        "norm2_g": 1.0 + nrm(ks[12], (DEPTH, D_MODEL), 0.05),
        "peer_wq": nrm(ks[13], (DEPTH, D_MODEL, PEER_HEADS * PEER_DKEY), D_MODEL ** -0.5),
        "peer_subkeys": nrm(ks[14], (DEPTH, PEER_HEADS, 2, N_KEYS, PEER_DHALF), PEER_DHALF ** -0.5),
        "peer_u": nrm(ks[15], (DEPTH, N_EXPERTS, D_MODEL), D_MODEL ** -0.5),
        "peer_v": nrm(ks[16], (DEPTH, N_EXPERTS, D_MODEL), 0.5),
        "final_norm_g": 1.0 + nrm(ks[17], (D_MODEL,), 0.05),
    }


def reference(x_prompt, x_sample, cache_k_win, cache_v_win, state_gla, norm1_g, w_in, w_gla_gate,
              b_gla_gate, attn_sinks, gla_norm_g, w_out, norm2_g, peer_wq, peer_subkeys, peer_u, peer_v,
              final_norm_g):
    y_p, y_s = x_prompt, x_sample
    kp_l, vp_l, sp_l, ks_l, vs_l, ss_l = [], [], [], [], [], []
    for l in range(DEPTH):
        w = (norm1_g[l], w_in[l], w_gla_gate[l], b_gla_gate[l], attn_sinks[l], gla_norm_g[l],
             w_out[l], norm2_g[l], peer_wq[l], peer_subkeys[l], peer_u[l], peer_v[l])
        S0_p = jnp.zeros((x_prompt.shape[0], HEADS_B, DK_B, DV_B), jnp.float32)
        y_p, kp, vp, sp = layer(y_p, True, None, None, S0_p, *w)
        y_s, k_s, v_s, s_s = layer(y_s, False, cache_k_win[l], cache_v_win[l], state_gla[l], *w)
        kp_l.append(kp.astype(cache_k_win.dtype)); vp_l.append(vp.astype(cache_v_win.dtype))
        sp_l.append(sp.astype(state_gla.dtype))
        ks_l.append(k_s.astype(cache_k_win.dtype)); vs_l.append(v_s.astype(cache_v_win.dtype))
        ss_l.append(s_s.astype(state_gla.dtype))
    y_prompt = rmsnorm(y_p, final_norm_g)
    y_sample = rmsnorm(y_s, final_norm_g)
    return (y_prompt, y_sample, jnp.stack(kp_l), jnp.stack(vp_l), jnp.stack(sp_l),
            jnp.stack(ks_l), jnp.stack(vs_l), jnp.stack(ss_l))
```

```python
import functools
import math

import numpy as np
import jax
import jax.numpy as jnp
from jax import lax
from jax.experimental import pallas as pl
from jax.experimental.pallas import tpu as pltpu

F32 = jnp.float32
BF16 = jnp.bfloat16

EPS = 1e-6
GATE_TEMP = 16.0
PEER_TOPK = 16
LOG2E = 1.4426950408889634
INV_SQRT2 = 0.7071067811865476

V7X_LANES = 128
V7X_SUBLANES = 8
V7X_VMEM_BYTES = 64 * 1024 * 1024
MIB = 1024 * 1024


def _vmem_limit(estimate_bytes):
    return int(min(estimate_bytes + 16 * MIB, V7X_VMEM_BYTES - 8 * MIB))


def _dot_nt(a, b):
    return lax.dot_general(a, b, (((1,), (1,)), ((), ())), preferred_element_type=F32)


def _sigmoid(z):
    return 1.0 / (1.0 + jnp.exp(-z))


def _rmsnorm_rows(x, g):
    ms = jnp.mean(x * x, axis=-1, keepdims=True)
    return x * lax.rsqrt(ms + EPS) * g


def _inproj_kernel(x_ref, g_ref, w_ref, proj_ref, kv_ref, *, chunks, kv_chunk):
    n = _rmsnorm_rows(x_ref[...], g_ref[...]).astype(BF16)
    for c0, c1 in chunks:
        acc = jnp.dot(n, w_ref[:, c0:c1], preferred_element_type=F32)
        proj_ref[:, c0:c1] = acc.astype(BF16)
        if (c0, c1) == kv_chunk:
            kv_ref[...] = acc


def _inproj(x2d, g, w, lay, tm):
    t, d = x2d.shape
    ncol = w.shape[1]
    kv0, kvw = lay["kv"]
    step = 512
    chunks = tuple((c, min(c + step, ncol)) for c in range(0, ncol, step))
    assert (kv0, kv0 + kvw) in chunks
    est = 2 * tm * d * 4 + 2 * d * ncol * 2 + 2 * tm * ncol * 2 + 2 * tm * kvw * 4
    return pl.pallas_call(
        functools.partial(_inproj_kernel, chunks=chunks, kv_chunk=(kv0, kv0 + kvw)),
        grid=(t // tm,),
        in_specs=[
            pl.BlockSpec((tm, d), lambda i: (i, 0)),
            pl.BlockSpec((1, d), lambda i: (0, 0)),
            pl.BlockSpec((d, ncol), lambda i: (0, 0)),
        ],
        out_specs=[
            pl.BlockSpec((tm, ncol), lambda i: (i, 0)),
            pl.BlockSpec((tm, kvw), lambda i: (i, 0)),
        ],
        out_shape=[
            jax.ShapeDtypeStruct((t, ncol), BF16),
            jax.ShapeDtypeStruct((t, kvw), F32),
        ],
        compiler_params=pltpu.CompilerParams(
            dimension_semantics=("parallel",), vmem_limit_bytes=_vmem_limit(est)),
        name="inproj",
    )(x2d, g.reshape(1, d), w)


def _swa_prompt_kernel(sink_ref, q_ref, kvc_ref, kvp_ref, o_ref, *, heads, kv_heads, hd, window, slopes):
    j = pl.program_id(1)
    group = heads // kv_heads
    kvd = kv_heads * hd
    kvc = kvc_ref[...]
    kvp = kvp_ref[...]
    k_all = jnp.concatenate([kvp[:, :kvd], kvc[:, :kvd]], axis=0).astype(BF16)
    v_all = jnp.concatenate([kvp[:, kvd:], kvc[:, kvd:]], axis=0).astype(BF16)
    t_idx = lax.broadcasted_iota(jnp.int32, (window, 2 * window), 0)
    s_idx = lax.broadcasted_iota(jnp.int32, (window, 2 * window), 1)
    dist = window + t_idx - s_idx
    valid = (dist >= 0) & (dist < window) & ((s_idx >= window) | (j > 0))
    distf = dist.astype(F32)
    scale = hd ** -0.5
    for h in range(heads):
        kvh = h // group
        qh = q_ref[:, h * hd:(h + 1) * hd]
        kh = k_all[:, kvh * hd:(kvh + 1) * hd]
        vh = v_all[:, kvh * hd:(kvh + 1) * hd]
        s = _dot_nt(qh, kh) * scale - slopes[h] * distf
        s = jnp.where(valid, s, -jnp.inf)
        sink = sink_ref[h]
        m = jnp.maximum(jnp.max(s, axis=-1, keepdims=True), sink)
        p = jnp.exp(s - m)
        denom = jnp.sum(p, axis=-1, keepdims=True) + jnp.exp(sink - m)
        oh = jnp.dot((p / denom).astype(BF16), vh, preferred_element_type=F32)
        o_ref[:, h * hd:(h + 1) * hd] = oh.astype(o_ref.dtype)


def _swa_prompt(proj, kv, sinks, lay, batch, seq, heads, kv_heads, hd, window):
    t = batch * seq
    nb = seq // window
    q0, qw = lay["qa"]
    assert q0 == 0
    kvw = kv.shape[1]
    slopes = tuple(float(2.0 ** (-8.0 * (h + 1) / heads)) for h in range(heads))
    est = 2 * window * qw * 2 * 2 + 4 * window * kvw * 4
    return pl.pallas_call(
        functools.partial(_swa_prompt_kernel, heads=heads, kv_heads=kv_heads, hd=hd,
                          window=window, slopes=slopes),
        grid=(batch, nb),
        in_specs=[
            pl.BlockSpec(memory_space=pltpu.SMEM),
            pl.BlockSpec((window, qw), lambda b, j: (b * nb + j, 0)),
            pl.BlockSpec((window, kvw), lambda b, j: (b * nb + j, 0)),
            pl.BlockSpec((window, kvw), lambda b, j: (b * nb + jnp.maximum(j - 1, 0), 0)),
        ],
        out_specs=pl.BlockSpec((window, qw), lambda b, j: (b * nb + j, 0)),
        out_shape=jax.ShapeDtypeStruct((t, qw), BF16),
        compiler_params=pltpu.CompilerParams(
            dimension_semantics=("parallel", "parallel"), vmem_limit_bytes=_vmem_limit(est)),
        name="swa_prompt",
    )(sinks, proj, kv, kv)


def _swa_sample_kernel(qrep_ref, ck_ref, cv_ref, kvn_ref, sink_ref, slope_ref, nk_ref, nv_ref, o_ref,
                       *, bs, kv_heads, hd, window):
    kvd = kv_heads * hd
    rows = qrep_ref.shape[1]
    row_idx = lax.broadcasted_iota(jnp.int32, (window, kvd), 0)
    lane_blk = lax.broadcasted_iota(jnp.int32, (rows, kvd), 1) // hd
    row_kv = lax.broadcasted_iota(jnp.int32, (rows, kvd), 0) % kv_heads
    head_mask = lane_blk == row_kv
    dist = (window - 1 - lax.broadcasted_iota(jnp.int32, (rows, window), 1)).astype(F32)
    sink = sink_ref[...]
    slope = slope_ref[...]
    scale = hd ** -0.5
    for i in range(bs):
        knew = kvn_ref[i:i + 1, :kvd]
        vnew = kvn_ref[i:i + 1, kvd:]
        nk = jnp.where(row_idx == window - 1, knew, pltpu.roll(ck_ref[i], window - 1, axis=0))
        nv = jnp.where(row_idx == window - 1, vnew, pltpu.roll(cv_ref[i], window - 1, axis=0))
        nk_ref[i] = nk
        nv_ref[i] = nv
        qe = jnp.where(head_mask, qrep_ref[i], jnp.zeros_like(qrep_ref[i]))
        s = _dot_nt(qe, nk.astype(BF16)) * scale - slope * dist
        m = jnp.maximum(jnp.max(s, axis=-1, keepdims=True), sink)
        p = jnp.exp(s - m)
        denom = jnp.sum(p, axis=-1, keepdims=True) + jnp.exp(sink - m)
        r = jnp.dot((p / denom).astype(BF16), nv.astype(BF16), preferred_element_type=F32)
        r = jnp.where(head_mask, r, 0.0)
        shift = 1
        while shift < kv_heads:
            r = r + pltpu.roll(r, rows - shift, axis=0)
            shift *= 2
        o_ref[i] = r


def _swa_sample(qrep, ck, cv, kvn, sink_col, slope_col, kv_heads, hd, window, bs=8):
    s, rows, kvd = qrep.shape
    est = 2 * bs * (rows * kvd * 2 + 4 * window * kvd * 4 + 2 * kvd * 4 + rows * kvd * 4)
    return pl.pallas_call(
        functools.partial(_swa_sample_kernel, bs=bs, kv_heads=kv_heads, hd=hd, window=window),
        grid=(s // bs,),
        in_specs=[
            pl.BlockSpec((bs, rows, kvd), lambda i: (i, 0, 0)),
            pl.BlockSpec((bs, window, kvd), lambda i: (i, 0, 0)),
            pl.BlockSpec((bs, window, kvd), lambda i: (i, 0, 0)),
            pl.BlockSpec((bs, 2 * kvd), lambda i: (i, 0)),
            pl.BlockSpec((rows, 1), lambda i: (0, 0)),
            pl.BlockSpec((rows, 1), lambda i: (0, 0)),
        ],
        out_specs=[
            pl.BlockSpec((bs, window, kvd), lambda i: (i, 0, 0)),
            pl.BlockSpec((bs, window, kvd), lambda i: (i, 0, 0)),
            pl.BlockSpec((bs, rows, kvd), lambda i: (i, 0, 0)),
        ],
        out_shape=[
            jax.ShapeDtypeStruct((s, window, kvd), F32),
            jax.ShapeDtypeStruct((s, window, kvd), F32),
            jax.ShapeDtypeStruct((s, rows, kvd), F32),
        ],
        compiler_params=pltpu.CompilerParams(
            dimension_semantics=("parallel",), vmem_limit_bytes=_vmem_limit(est)),
        name="swa_sample",
    )(qrep, ck, cv, kvn, sink_col, slope_col)


def _gla_tables(c):
    nlev = int(round(math.log2(c)))
    assert 1 << nlev == c
    t = np.arange(c)
    u = t[None, :]
    rows = [u <= t[:, None], u > t[:, None]]
    lvl = np.full((c, c), -1, np.int32)
    lvl[t, t] = 0
    for l in range(1, nlev + 1):
        hs = 1 << (l - 1)
        pos = t % (2 * hs)
        right = pos >= hs
        bnd = (t - pos + hs - 1)[:, None]
        rows.append(right[:, None] & (u > bnd) & (u <= t[:, None]))
        rows.append((~right)[:, None] & (u > t[:, None]) & (u <= bnd))
        same = (t[:, None] // (2 * hs)) == (t[None, :] // (2 * hs))
        lvl[same & right[:, None] & (~right)[None, :]] = l
    mall = np.concatenate(rows, axis=0).astype(np.float32)
    return jnp.asarray(mall, dtype=BF16), jnp.asarray(lvl), nlev


def _gla_kernel(*refs, c, nlev, heads, dk, dv, sample_mode):
    if sample_mode:
        (q_ref, k_ref, v_ref, r_ref, z_ref, wg_ref, bg_ref, gn_ref, mall_ref, lvl_ref, s0_ref,
         og_ref, sout_ref, s_scr) = refs
    else:
        (q_ref, k_ref, v_ref, r_ref, z_ref, wg_ref, bg_ref, gn_ref, mall_ref, lvl_ref,
         og_ref, sout_ref, s_scr) = refs
    step = pl.program_id(1)

    if sample_mode:
        r0 = pl.program_id(0) % c
        row_mask = lax.broadcasted_iota(jnp.int32, (c, 1), 0) == r0
        s_scr[...] = s0_ref[0]
    else:
        @pl.when(step == 0)
        def _():
            s_scr[...] = jnp.zeros_like(s_scr)

    z = jnp.dot(z_ref[...], wg_ref[...], preferred_element_type=F32) + bg_ref[...]
    la_all = -(jnp.maximum(-z, 0.0) + jnp.log(1.0 + jnp.exp(-jnp.abs(z)))) / GATE_TEMP
    if sample_mode:
        la_all = jnp.where(row_mask, la_all, 0.0)
    lvl = lvl_ref[...]
    mall = mall_ref[...]
    gn = gn_ref[...]
    outs = []
    for hh in range(heads):
        la = la_all[:, hh * dk:(hh + 1) * dk]
        la_hi = la.astype(BF16)
        la_lo = (la - la_hi.astype(F32)).astype(BF16)
        e2 = jnp.dot(mall, jnp.concatenate([la_hi, la_lo], axis=1), preferred_element_type=F32)
        e = e2[:, :dk] + e2[:, dk:]
        b = e[0:c]
        q = q_ref[:, hh * dk:(hh + 1) * dk].astype(F32) * (dk ** -0.5)
        k = k_ref[:, hh * dk:(hh + 1) * dk].astype(F32)
        v = v_ref[:, hh * dv:(hh + 1) * dv]
        if sample_mode:
            k = jnp.where(row_mask, k, 0.0)
            v = jnp.where(row_mask, v, jnp.zeros_like(v))
        scores = jnp.where(lvl == 0, _dot_nt(q.astype(BF16), k.astype(BF16)), 0.0)
        for l in range(1, nlev + 1):
            eq = jnp.exp(e[2 * l * c:(2 * l + 1) * c])
            ek = jnp.exp(e[(2 * l + 1) * c:(2 * l + 2) * c])
            sc = _dot_nt((q * eq).astype(BF16), (k * ek).astype(BF16))
            scores = scores + jnp.where(lvl == l, sc, 0.0)
        s_prev = s_scr[hh]
        o = jnp.dot(scores.astype(BF16), v, preferred_element_type=F32)
        o = o + jnp.dot((q * jnp.exp(b)).astype(BF16), s_prev.astype(BF16), preferred_element_type=F32)
        dec = jnp.transpose(jnp.broadcast_to(jnp.exp(b[c - 1:c, :]), (dk, dk)))
        dec = jnp.concatenate([dec] * (dv // dk), axis=1)
        khat_t = jnp.transpose(k * jnp.exp(e[c:2 * c])).astype(BF16)
        s_scr[hh] = dec * s_prev + jnp.dot(khat_t, v, preferred_element_type=F32)
        og = _rmsnorm_rows(o, gn)
        rg = r_ref[:, hh * dv:(hh + 1) * dv].astype(F32)
        outs.append(og * (rg * _sigmoid(rg)))
    og_all = jnp.concatenate(outs, axis=1)
    if sample_mode:
        @pl.when(r0 == 0)
        def _():
            og_ref[...] = jnp.zeros_like(og_ref)
        og_ref[...] = jnp.where(row_mask, og_all, og_ref[...])
        sout_ref[0] = s_scr[...]
    else:
        og_ref[...] = og_all.astype(og_ref.dtype)

        @pl.when(step == pl.num_programs(1) - 1)
        def _():
            sout_ref[0] = s_scr[...]


def _gla(proj, wg, bg, gn, lay, heads, dk, dv, *, batch, seq, chunk, s0=None):
    sample_mode = s0 is not None
    t = proj.shape[0]
    mall, lvl, nlev = _gla_tables(chunk)
    q0, qw = lay["qb"]
    k0, kw = lay["kb"]
    v0, vw = lay["vb"]
    r0_, rw = lay["rb"]
    z0, zw = lay["z"]
    if sample_mode:
        grid = (t, 1)
        row = lambda b, s: b // chunk
    else:
        nchunk = seq // chunk
        grid = (batch, nchunk)
        row = lambda b, s: b * nchunk + s
    in_specs = [
        pl.BlockSpec((chunk, qw), lambda b, s: (row(b, s), q0 // qw)),
        pl.BlockSpec((chunk, kw), lambda b, s: (row(b, s), k0 // kw)),
        pl.BlockSpec((chunk, vw), lambda b, s: (row(b, s), v0 // vw)),
        pl.BlockSpec((chunk, rw), lambda b, s: (row(b, s), r0_ // rw)),
        pl.BlockSpec((chunk, zw), lambda b, s: (row(b, s), z0 // zw)),
        pl.BlockSpec(wg.shape, lambda b, s: (0, 0)),
        pl.BlockSpec((1, heads * dk), lambda b, s: (0, 0)),
        pl.BlockSpec((1, dv), lambda b, s: (0, 0)),
        pl.BlockSpec(mall.shape, lambda b, s: (0, 0)),
        pl.BlockSpec(lvl.shape, lambda b, s: (0, 0)),
    ]
    args = [proj, proj, proj, proj, proj, wg, bg.reshape(1, -1), gn.reshape(1, -1), mall, lvl]
    nstate = t if sample_mode else batch
    if sample_mode:
        in_specs.append(pl.BlockSpec((1, heads, dk, dv), lambda b, s: (b, 0, 0, 0)))
        args.append(s0)
    og_dtype = F32 if sample_mode else BF16
    est = (2 * chunk * (qw + kw + vw + rw + zw) * 2 + 2 * mall.size * 2 + 2 * lvl.size * 4
           + 5 * heads * dk * dv * 4 + 2 * chunk * vw * 4 + 8 * mall.shape[0] * 2 * dk * 4)
    return pl.pallas_call(
        functools.partial(_gla_kernel, c=chunk, nlev=nlev, heads=heads, dk=dk, dv=dv,
                          sample_mode=sample_mode),
        grid=grid,
        in_specs=in_specs,
        out_specs=[
            pl.BlockSpec((chunk, heads * dv), lambda b, s: (row(b, s), 0)),
            pl.BlockSpec((1, heads, dk, dv), lambda b, s: (b, 0, 0, 0)),
        ],
        out_shape=[
            jax.ShapeDtypeStruct((t, heads * dv), og_dtype),
            jax.ShapeDtypeStruct((nstate, heads, dk, dv), F32),
        ],
        scratch_shapes=[pltpu.VMEM((heads, dk, dv), F32)],
        compiler_params=pltpu.CompilerParams(
            dimension_semantics=("arbitrary", "arbitrary"), vmem_limit_bytes=_vmem_limit(est)),
        name="gla_sample" if sample_mode else "gla_prompt",
    )(*args)


def _post_kernel(x_ref, oa_ref, og_ref, ga_ref, gb_ref, wo_ref, g2_ref, wq_ref, h_ref, n2_ref, pq_ref):
    u = (_sigmoid(ga_ref[...].astype(F32)) * oa_ref[...].astype(F32)
         + _sigmoid(gb_ref[...].astype(F32)) * og_ref[...].astype(F32))
    h = x_ref[...] + jnp.dot(u.astype(BF16), wo_ref[...], preferred_element_type=F32)
    h_ref[...] = h
    n2 = _rmsnorm_rows(h, g2_ref[...]).astype(BF16)
    n2_ref[...] = n2
    pq_ref[...] = jnp.dot(n2, wq_ref[...], preferred_element_type=F32).astype(BF16)


def _post(x2d, oa, og, proj, wo, g2, wq, lay, tm):
    t, d = x2d.shape
    ga0, gaw = lay["ga"]
    gb0, gbw = lay["gb"]
    nq = wq.shape[1]
    est = (2 * tm * d * (4 + oa.dtype.itemsize + og.dtype.itemsize + 2 + 2 + 4 + 2)
           + 2 * d * d * 2 + 2 * d * nq * 2 + 2 * tm * nq * 2 + tm * nq * 4)
    return pl.pallas_call(
        _post_kernel,
        grid=(t // tm,),
        in_specs=[
            pl.BlockSpec((tm, d), lambda i: (i, 0)),
            pl.BlockSpec((tm, d), lambda i: (i, 0)),
            pl.BlockSpec((tm, d), lambda i: (i, 0)),
            pl.BlockSpec((tm, gaw), lambda i: (i, ga0 // gaw)),
            pl.BlockSpec((tm, gbw), lambda i: (i, gb0 // gbw)),
            pl.BlockSpec((d, d), lambda i: (0, 0)),
            pl.BlockSpec((1, d), lambda i: (0, 0)),
            pl.BlockSpec((d, nq), lambda i: (0, 0)),
        ],
        out_specs=[
            pl.BlockSpec((tm, d), lambda i: (i, 0)),
            pl.BlockSpec((tm, d), lambda i: (i, 0)),
            pl.BlockSpec((tm, nq), lambda i: (i, 0)),
        ],
        out_shape=[
            jax.ShapeDtypeStruct((t, d), F32),
            jax.ShapeDtypeStruct((t, d), BF16),
            jax.ShapeDtypeStruct((t, nq), BF16),
        ],
        compiler_params=pltpu.CompilerParams(
            dimension_semantics=("parallel",), vmem_limit_bytes=_vmem_limit(est)),
        name="post",
    )(x2d, oa, og, proj, proj, wo, g2.reshape(1, d), wq)


def _oddeven_mergesort_pairs(n):
    pairs = []
    p = 1
    while p < n:
        k = p
        while k >= 1:
            for j in range(k % p, n - k, 2 * k):
                for i in range(min(k, n - j - k)):
                    if (i + j) // (2 * p) == (i + j + k) // (2 * p):
                        pairs.append((i + j, i + j + k))
            k //= 2
        p *= 2
    return pairs


_SORT16_PAIRS = tuple(_oddeven_mergesort_pairs(PEER_TOPK))


def _cmpx(xs, i, j):
    a, b = xs[i], xs[j]
    if b is None:
        return
    if a is None:
        xs[i], xs[j] = b, None
        return
    xs[i], xs[j] = jnp.maximum(a, b), jnp.minimum(a, b)


def _bitonic_sort_desc(xs):
    n = len(xs)
    k = n // 2
    while k >= 1:
        for i in range(n):
            if i & k == 0:
                _cmpx(xs, i, i + k)
        k //= 2
    return xs


def _top_set(x, y):
    n = PEER_TOPK
    x = list(x) + [None] * (n - len(x))
    y = list(y) + [None] * (n - len(y))
    out = []
    for i in range(n):
        a, b = x[i], y[n - 1 - i]
        out.append(b if a is None else (a if b is None else jnp.maximum(a, b)))
    return out


def _merge_desc(x, y):
    return [v for v in _bitonic_sort_desc(_top_set(x, y)) if v is not None]


def _top16_of_keys(s_t):
    nslab = s_t.shape[0] // V7X_SUBLANES
    assert nslab == PEER_TOPK
    xs = [s_t[i * V7X_SUBLANES:(i + 1) * V7X_SUBLANES, :] for i in range(nslab)]
    for i, j in _SORT16_PAIRS:
        _cmpx(xs, i, j)
    shift = V7X_SUBLANES // 2
    while shift >= 1:
        ys = [pltpu.roll(xs[PEER_TOPK - 1 - i], shift, axis=0) for i in range(PEER_TOPK)]
        xs = _bitonic_sort_desc([jnp.maximum(a, b) for a, b in zip(xs, ys)])
        shift //= 2
    return xs


def _pair_sum_top_set(a, b):
    n = PEER_TOPK
    row = lambda i, j0, j1: [a[i] + b[j] for j in range(j0, j1)]
    col = lambda j, i0, i1: [a[i] + b[j] for i in range(i0, i1)]
    top_rows = _merge_desc(_merge_desc(row(0, 0, n), row(1, 0, n // 2)),
                           _merge_desc(row(2, 0, n // 3), row(3, 0, n // 4)))
    rest = _merge_desc(_merge_desc(col(0, 4, n), col(1, 4, n // 2)), col(2, 4, n // 3))
    return _top_set(top_rows, rest)


def _peer_kernel(n2_ref, pq_ref, h_ref, sk_ref, u_ref, vt_ref, fg_ref, y_ref,
                 s2_scr, d1_scr, thr_scr, w_scr, acc_scr, *, heads, nkeys, tb, ec, final_norm):
    j = pl.program_id(1)
    ln = V7X_LANES
    ntile = tb // ln
    sub = lax.broadcasted_iota(jnp.int32, (V7X_SUBLANES, ln), 0)
    assert heads == V7X_SUBLANES

    @pl.when(j == 0)
    def _route():
        acc_scr[...] = jnp.zeros_like(acc_scr)

        def tile_body(lt, carry):
            l0 = pl.multiple_of(lt * ln, ln)
            a_pack = [None] * PEER_TOPK
            b_pack = [None] * PEER_TOPK
            for h in range(heads):
                for c in range(2):
                    col = (h * 2 + c) * nkeys
                    qt = pq_ref[pl.ds(l0, ln), col:col + nkeys]
                    s_t = _dot_nt(sk_ref[h * 2 + c], qt) * LOG2E
                    if c == 0:
                        d1_scr[h, :, pl.ds(l0, ln)] = s_t
                    else:
                        s2_scr[h, :, pl.ds(l0, ln)] = s_t
                    top = _top16_of_keys(s_t)
                    pack = a_pack if c == 0 else b_pack
                    for r in range(PEER_TOPK):
                        pack[r] = top[r] if h == 0 else jnp.where(sub == h, top[r], pack[r])
            top_s = _pair_sum_top_set(a_pack, b_pack)
            m = a_pack[0] + b_pack[0]
            zsum = None
            for v in top_s:
                e = jnp.exp2(v - m)
                zsum = e if zsum is None else zsum + e
            shift = m + jnp.log2(zsum)
            d_pack = [a - shift for a in a_pack]
            top_u = _pair_sum_top_set(d_pack, b_pack)
            thr = top_u[0]
            for v in top_u[1:]:
                thr = jnp.minimum(thr, v)
            thr_scr[:, pl.ds(l0, ln)] = thr
            for h in range(heads):
                d1_scr[h, :, pl.ds(l0, ln)] = d1_scr[h, :, pl.ds(l0, ln)] - shift[h:h + 1, :]
            return carry

        lax.fori_loop(0, ntile, tile_body, 0)

    a_t = _dot_nt(u_ref[...], n2_ref[...])
    gl = 0.5 * a_t * (1.0 + lax.erf(a_t * INV_SQRT2))
    n_i1 = ec // nkeys
    assert n_i1 == V7X_SUBLANES
    i1_base = pl.multiple_of(j * n_i1, n_i1)
    for il in range(n_i1):
        for lt in range(ntile):
            lanes = slice(lt * ln, (lt + 1) * ln)
            gate = jnp.zeros((nkeys, ln), F32)
            for h in range(heads):
                d1_rows = d1_scr[h, pl.ds(i1_base, n_i1), lanes]
                u = s2_scr[h, :, lanes] + d1_rows[il:il + 1, :]
                gate = gate + jnp.where(u >= thr_scr[h:h + 1, lanes], jnp.exp2(u), 0.0)
            w_scr[il * nkeys:(il + 1) * nkeys, lanes] = (
                gate * gl[il * nkeys:(il + 1) * nkeys, lanes]).astype(BF16)
    acc_scr[...] += jnp.dot(vt_ref[...], w_scr[...], preferred_element_type=F32)

    @pl.when(j == pl.num_programs(1) - 1)
    def _finish():
        xo = h_ref[...] + jnp.transpose(acc_scr[...])
        y_ref[...] = _rmsnorm_rows(xo, fg_ref[...]) if final_norm else xo


def _peer(n2, pq, h, sk, u_bf, vt_bf, fg, *, heads, nkeys, tb, ec, final_norm):
    t, d = h.shape
    ne = u_bf.shape[0]
    nq = pq.shape[1]
    est = (2 * tb * d * 2 + 2 * tb * nq * 2 + 2 * tb * d * 4 + 2 * sk.size * 2 + 4 * ec * d * 2
           + 2 * tb * d * 4 + 2 * heads * nkeys * tb * 4 + 8 * tb * 4 + ec * tb * 2 + d * tb * 4
           + 3 * ec * tb * 4)
    return pl.pallas_call(
        functools.partial(_peer_kernel, heads=heads, nkeys=nkeys, tb=tb, ec=ec, final_norm=final_norm),
        grid=(t // tb, ne // ec),
        in_specs=[
            pl.BlockSpec((tb, d), lambda i, j: (i, 0)),
            pl.BlockSpec((tb, nq), lambda i, j: (i, 0)),
            pl.BlockSpec((tb, d), lambda i, j: (i, 0)),
            pl.BlockSpec(sk.shape, lambda i, j: (0, 0, 0)),
            pl.BlockSpec((ec, d), lambda i, j: (j, 0)),
            pl.BlockSpec((d, ec), lambda i, j: (0, j)),
            pl.BlockSpec((1, d), lambda i, j: (0, 0)),
        ],
        out_specs=pl.BlockSpec((tb, d), lambda i, j: (i, 0)),
        out_shape=jax.ShapeDtypeStruct((t, d), F32),
        scratch_shapes=[
            pltpu.VMEM((heads, nkeys, tb), F32),
            pltpu.VMEM((heads, nkeys, tb), F32),
            pltpu.VMEM((heads, tb), F32),
            pltpu.VMEM((ec, tb), BF16),
            pltpu.VMEM((d, tb), F32),
        ],
        compiler_params=pltpu.CompilerParams(
            dimension_semantics=("parallel", "arbitrary"), vmem_limit_bytes=_vmem_limit(est)),
        name="peer",
    )(n2, pq, h, sk, u_bf, vt_bf, fg.reshape(1, d))


def _layout(d, heads_a, kv_a, hd_a, heads_b, dk, dv, rank):
    widths = [("qa", heads_a * hd_a), ("vb", heads_b * dv), ("rb", heads_b * dv), ("ga", d), ("gb", d),
              ("qb", heads_b * dk), ("kb", heads_b * dk), ("kv", 2 * kv_a * hd_a), ("z", V7X_LANES)]
    lay, off = {}, 0
    for name, w in widths:
        assert off % w == 0
        lay[name] = (off, w)
        off += w
    assert rank <= V7X_LANES
    lay["ncol"] = off
    return lay


def _prep_w_in(w, cols, rank):
    qa, ka, va, qb, kb, vb, rb, zb, ga, gb = jnp.split(w, np.cumsum(cols)[:-1].tolist(), axis=1)
    zpad = jnp.pad(zb, ((0, 0), (0, V7X_LANES - rank)))
    return jnp.concatenate([qa, vb, rb, ga, gb, qb, kb, ka, va, zpad], axis=1).astype(BF16)


def kernel(x_prompt, x_sample, cache_k_win, cache_v_win, state_gla, norm1_g, w_in, w_gla_gate, b_gla_gate,
           attn_sinks, gla_norm_g, w_out, norm2_g, peer_wq, peer_subkeys, peer_u, peer_v, final_norm_g):
    batch, seq, d = x_prompt.shape
    nsamp, dec_seq, _ = x_sample.shape
    assert dec_seq == 1
    depth, _, window, kv_a, hd_a = cache_k_win.shape
    heads_a = attn_sinks.shape[1]
    group = heads_a // kv_a
    _, _, heads_b, dk, dv = state_gla.shape
    rank = w_gla_gate.shape[1]
    peer_heads, _, nkeys, dhalf = peer_subkeys.shape[1:]
    n_exp = peer_u.shape[1]
    assert n_exp == nkeys * nkeys and dhalf == nkeys == V7X_LANES
    cols = (heads_a * hd_a, kv_a * hd_a, kv_a * hd_a, heads_b * dk, heads_b * dk, heads_b * dv,
            heads_b * dv, rank, d, d)
    lay = _layout(d, heads_a, kv_a, hd_a, heads_b, dk, dv, rank)
    kvd = kv_a * hd_a

    yp = x_prompt.reshape(batch * seq, d)
    ys = x_sample.reshape(nsamp, d)
    outs = {k: [] for k in ("kp", "vp", "sp", "ks", "vs", "ss")}
    for l in range(depth):
        w_l = _prep_w_in(w_in[l], cols, rank)
        wg = jnp.pad(w_gla_gate[l], ((0, V7X_LANES - rank), (0, 0))).astype(BF16)
        wo = w_out[l].astype(BF16)
        wq = peer_wq[l].astype(BF16)
        sk = peer_subkeys[l].reshape(peer_heads * 2, nkeys, dhalf).astype(BF16)
        u_bf = peer_u[l].astype(BF16)
        vt_bf = jnp.transpose(peer_v[l]).astype(BF16)
        last = l == depth - 1

        proj_p, kv_p = _inproj(yp, norm1_g[l], w_l, lay, tm=512)
        oa_p = _swa_prompt(proj_p, kv_p, attn_sinks[l], lay, batch, seq, heads_a, kv_a, hd_a, window)
        og_p, s_p = _gla(proj_p, wg, b_gla_gate[l], gla_norm_g[l], lay, heads_b, dk, dv,
                         batch=batch, seq=seq, chunk=128)
        h_p, n2_p, pq_p = _post(yp, oa_p, og_p, proj_p, wo, norm2_g[l], wq, lay, tm=512)
        yp = _peer(n2_p, pq_p, h_p, sk, u_bf, vt_bf, final_norm_g, heads=peer_heads, nkeys=nkeys,
                   tb=512, ec=V7X_SUBLANES * nkeys, final_norm=last)
        kv_last = kv_p.reshape(batch, seq, 2 * kvd)[:, seq - window:, :]
        outs["kp"].append(kv_last[..., :kvd].reshape(batch, window, kv_a, hd_a))
        outs["vp"].append(kv_last[..., kvd:].reshape(batch, window, kv_a, hd_a))
        outs["sp"].append(s_p)

        proj_s, kv_s = _inproj(ys, norm1_g[l], w_l, lay, tm=nsamp)
        qa_s = proj_s[:, :heads_a * hd_a].reshape(nsamp, kv_a, group, hd_a)
        qrep = jnp.transpose(qa_s, (0, 2, 1, 3)).reshape(nsamp, group, 1, kvd)
        qrep = jnp.broadcast_to(qrep, (nsamp, group, kv_a, kvd)).reshape(nsamp, heads_a, kvd)
        row_head = np.array([(r % kv_a) * group + r // kv_a for r in range(heads_a)])
        sink_col = attn_sinks[l][row_head].reshape(heads_a, 1)
        slope_col = jnp.asarray(2.0 ** (-8.0 * (row_head + 1) / heads_a), F32).reshape(heads_a, 1)
        nk_s, nv_s, o_rows = _swa_sample(
            qrep, cache_k_win[l].reshape(nsamp, window, kvd), cache_v_win[l].reshape(nsamp, window, kvd),
            kv_s, sink_col, slope_col, kv_a, hd_a, window)
        oa_s = o_rows[:, ::kv_a, :].reshape(nsamp, group, kv_a, hd_a)
        oa_s = jnp.transpose(oa_s, (0, 2, 1, 3)).reshape(nsamp, d)
        og_s, s_s = _gla(proj_s, wg, b_gla_gate[l], gla_norm_g[l], lay, heads_b, dk, dv,
                         batch=nsamp, seq=1, chunk=V7X_SUBLANES, s0=state_gla[l])
        h_s, n2_s, pq_s = _post(ys, oa_s, og_s, proj_s, wo, norm2_g[l], wq, lay, tm=nsamp)
        ys = _peer(n2_s, pq_s, h_s, sk, u_bf, vt_bf, final_norm_g, heads=peer_heads, nkeys=nkeys,
                   tb=nsamp, ec=V7X_SUBLANES * nkeys, final_norm=last)
        outs["ks"].append(nk_s.reshape(nsamp, window, kv_a, hd_a))
        outs["vs"].append(nv_s.reshape(nsamp, window, kv_a, hd_a))
        outs["ss"].append(s_s)

    return (yp.reshape(batch, seq, d), ys.reshape(nsamp, 1, d),
            jnp.stack(outs["kp"]), jnp.stack(outs["vp"]), jnp.stack(outs["sp"]),
            jnp.stack(outs["ks"]), jnp.stack(outs["vs"]), jnp.stack(outs["ss"]))
```

```python
import functools
import math

import numpy as np
import jax
import jax.numpy as jnp
from jax import lax
from jax.experimental import pallas as pl
from jax.experimental.pallas import tpu as pltpu

F32 = jnp.float32
BF16 = jnp.bfloat16

EPS = 1e-6
GATE_TEMP = 16.0
PEER_TOPK = 16
LOG2E = 1.4426950408889634
INV_SQRT2 = 0.7071067811865476

V7X_LANES = 128
V7X_SUBLANES = 8
V7X_MXU_DIM = 256
V7X_VMEM_BYTES = 64 * 1024 * 1024
MIB = 1024 * 1024


def _vmem_limit(estimate_bytes):
    return int(min(estimate_bytes + 16 * MIB, V7X_VMEM_BYTES - 8 * MIB))


def _dot_nt(a, b):
    return lax.dot_general(a, b, (((1,), (1,)), ((), ())), preferred_element_type=F32)


def _sigmoid(z):
    return 1.0 / (1.0 + jnp.exp(-z))


def _rmsnorm_rows(x, g):
    ms = jnp.mean(x * x, axis=-1, keepdims=True)
    return x * lax.rsqrt(ms + EPS) * g


def _inproj_kernel(x_ref, g_ref, w_ref, proj_ref, kv_ref, *, chunks, kv_chunk):
    n = _rmsnorm_rows(x_ref[...], g_ref[...]).astype(BF16)
    for c0, c1 in chunks:
        acc = jnp.dot(n, w_ref[:, c0:c1], preferred_element_type=F32)
        proj_ref[:, c0:c1] = acc.astype(BF16)
        if (c0, c1) == kv_chunk:
            kv_ref[...] = acc


def _inproj(x2d, g, w, lay, tm):
    t, d = x2d.shape
    ncol = w.shape[1]
    kv0, kvw = lay["kv"]
    step = 512
    chunks = tuple((c, min(c + step, ncol)) for c in range(0, ncol, step))
    assert (kv0, kv0 + kvw) in chunks
    est = 2 * tm * d * 4 + 2 * d * ncol * 2 + 2 * tm * ncol * 2 + 2 * tm * kvw * 4
    return pl.pallas_call(
        functools.partial(_inproj_kernel, chunks=chunks, kv_chunk=(kv0, kv0 + kvw)),
        grid=(t // tm,),
        in_specs=[
            pl.BlockSpec((tm, d), lambda i: (i, 0)),
            pl.BlockSpec((1, d), lambda i: (0, 0)),
            pl.BlockSpec((d, ncol), lambda i: (0, 0)),
        ],
        out_specs=[
            pl.BlockSpec((tm, ncol), lambda i: (i, 0)),
            pl.BlockSpec((tm, kvw), lambda i: (i, 0)),
        ],
        out_shape=[
            jax.ShapeDtypeStruct((t, ncol), BF16),
            jax.ShapeDtypeStruct((t, kvw), F32),
        ],
        compiler_params=pltpu.CompilerParams(
            dimension_semantics=("parallel",), vmem_limit_bytes=_vmem_limit(est)),
        name="inproj",
    )(x2d, g.reshape(1, d), w)


def _swa_prompt_kernel(sink_ref, q_ref, kvc_ref, kvp_ref, o_ref, *, heads, kv_heads, hd, window, slopes):
    j = pl.program_id(1)
    group = heads // kv_heads
    kvd = kv_heads * hd
    kvc = kvc_ref[...]
    kvp = kvp_ref[...]
    k_all = jnp.concatenate([kvp[:, :kvd], kvc[:, :kvd]], axis=0).astype(BF16)
    v_all = jnp.concatenate([kvp[:, kvd:], kvc[:, kvd:]], axis=0).astype(BF16)
    t_idx = lax.broadcasted_iota(jnp.int32, (window, 2 * window), 0)
    s_idx = lax.broadcasted_iota(jnp.int32, (window, 2 * window), 1)
    dist = window + t_idx - s_idx
    valid = (dist >= 0) & (dist < window) & ((s_idx >= window) | (j > 0))
    distf = dist.astype(F32)
    scale = hd ** -0.5
    for h in range(heads):
        kvh = h // group
        qh = q_ref[:, h * hd:(h + 1) * hd]
        kh = k_all[:, kvh * hd:(kvh + 1) * hd]
        vh = v_all[:, kvh * hd:(kvh + 1) * hd]
        s = _dot_nt(qh, kh) * scale - slopes[h] * distf
        s = jnp.where(valid, s, -jnp.inf)
        sink = sink_ref[h]
        m = jnp.maximum(jnp.max(s, axis=-1, keepdims=True), sink)
        p = jnp.exp(s - m)
        denom = jnp.sum(p, axis=-1, keepdims=True) + jnp.exp(sink - m)
        oh = jnp.dot((p / denom).astype(BF16), vh, preferred_element_type=F32)
        o_ref[:, h * hd:(h + 1) * hd] = oh.astype(o_ref.dtype)


def _swa_prompt(proj, kv, sinks, lay, batch, seq, heads, kv_heads, hd, window):
    t = batch * seq
    nb = seq // window
    q0, qw = lay["qa"]
    assert q0 == 0
    kvw = kv.shape[1]
    slopes = tuple(float(2.0 ** (-8.0 * (h + 1) / heads)) for h in range(heads))
    est = 2 * window * qw * 2 * 2 + 4 * window * kvw * 4
    return pl.pallas_call(
        functools.partial(_swa_prompt_kernel, heads=heads, kv_heads=kv_heads, hd=hd,
                          window=window, slopes=slopes),
        grid=(batch, nb),
        in_specs=[
            pl.BlockSpec(memory_space=pltpu.SMEM),
            pl.BlockSpec((window, qw), lambda b, j: (b * nb + j, 0)),
            pl.BlockSpec((window, kvw), lambda b, j: (b * nb + j, 0)),
            pl.BlockSpec((window, kvw), lambda b, j: (b * nb + jnp.maximum(j - 1, 0), 0)),
        ],
        out_specs=pl.BlockSpec((window, qw), lambda b, j: (b * nb + j, 0)),
        out_shape=jax.ShapeDtypeStruct((t, qw), BF16),
        compiler_params=pltpu.CompilerParams(
            dimension_semantics=("parallel", "parallel"), vmem_limit_bytes=_vmem_limit(est)),
        name="swa_prompt",
    )(sinks, proj, kv, kv)


def _swa_sample_kernel(qrep_ref, ck_ref, cv_ref, kvn_ref, sink_ref, slope_ref, nk_ref, nv_ref, o_ref,
                       *, bs, kv_heads, hd, window):
    kvd = kv_heads * hd
    rows = qrep_ref.shape[1]
    row_idx = lax.broadcasted_iota(jnp.int32, (window, kvd), 0)
    lane_blk = lax.broadcasted_iota(jnp.int32, (rows, kvd), 1) // hd
    row_kv = lax.broadcasted_iota(jnp.int32, (rows, kvd), 0) % kv_heads
    head_mask = lane_blk == row_kv
    dist = (window - 1 - lax.broadcasted_iota(jnp.int32, (rows, window), 1)).astype(F32)
    sink = sink_ref[...]
    slope = slope_ref[...]
    scale = hd ** -0.5
    for i in range(bs):
        knew = kvn_ref[i:i + 1, :kvd]
        vnew = kvn_ref[i:i + 1, kvd:]
        nk = jnp.where(row_idx == window - 1, knew, pltpu.roll(ck_ref[i], window - 1, axis=0))
        nv = jnp.where(row_idx == window - 1, vnew, pltpu.roll(cv_ref[i], window - 1, axis=0))
        nk_ref[i] = nk
        nv_ref[i] = nv
        qe = jnp.where(head_mask, qrep_ref[i], jnp.zeros_like(qrep_ref[i]))
        s = _dot_nt(qe, nk.astype(BF16)) * scale - slope * dist
        m = jnp.maximum(jnp.max(s, axis=-1, keepdims=True), sink)
        p = jnp.exp(s - m)
        denom = jnp.sum(p, axis=-1, keepdims=True) + jnp.exp(sink - m)
        r = jnp.dot((p / denom).astype(BF16), nv.astype(BF16), preferred_element_type=F32)
        r = jnp.where(head_mask, r, 0.0)
        shift = 1
        while shift < kv_heads:
            r = r + pltpu.roll(r, rows - shift, axis=0)
            shift *= 2
        o_ref[i] = r


def _swa_sample(qrep, ck, cv, kvn, sink_col, slope_col, kv_heads, hd, window, bs=8):
    s, rows, kvd = qrep.shape
    est = 2 * bs * (rows * kvd * 2 + 4 * window * kvd * 4 + 2 * kvd * 4 + rows * kvd * 4)
    return pl.pallas_call(
        functools.partial(_swa_sample_kernel, bs=bs, kv_heads=kv_heads, hd=hd, window=window),
        grid=(s // bs,),
        in_specs=[
            pl.BlockSpec((bs, rows, kvd), lambda i: (i, 0, 0)),
            pl.BlockSpec((bs, window, kvd), lambda i: (i, 0, 0)),
            pl.BlockSpec((bs, window, kvd), lambda i: (i, 0, 0)),
            pl.BlockSpec((bs, 2 * kvd), lambda i: (i, 0)),
            pl.BlockSpec((rows, 1), lambda i: (0, 0)),
            pl.BlockSpec((rows, 1), lambda i: (0, 0)),
        ],
        out_specs=[
            pl.BlockSpec((bs, window, kvd), lambda i: (i, 0, 0)),
            pl.BlockSpec((bs, window, kvd), lambda i: (i, 0, 0)),
            pl.BlockSpec((bs, rows, kvd), lambda i: (i, 0, 0)),
        ],
        out_shape=[
            jax.ShapeDtypeStruct((s, window, kvd), F32),
            jax.ShapeDtypeStruct((s, window, kvd), F32),
            jax.ShapeDtypeStruct((s, rows, kvd), F32),
        ],
        compiler_params=pltpu.CompilerParams(
            dimension_semantics=("parallel",), vmem_limit_bytes=_vmem_limit(est)),
        name="swa_sample",
    )(qrep, ck, cv, kvn, sink_col, slope_col)


def _gla_tables(c):
    nlev = int(round(math.log2(c)))
    assert 1 << nlev == c
    t = np.arange(c)
    u = t[None, :]
    rows = [u <= t[:, None], u > t[:, None]]
    lvl = np.full((c, c), -1, np.int32)
    lvl[t, t] = 0
    for l in range(1, nlev + 1):
        hs = 1 << (l - 1)
        pos = t % (2 * hs)
        right = pos >= hs
        bnd = (t - pos + hs - 1)[:, None]
        rows.append(right[:, None] & (u > bnd) & (u <= t[:, None]))
        rows.append((~right)[:, None] & (u > t[:, None]) & (u <= bnd))
        same = (t[:, None] // (2 * hs)) == (t[None, :] // (2 * hs))
        lvl[same & right[:, None] & (~right)[None, :]] = l
    mall = np.concatenate(rows, axis=0).astype(np.float32)
    return jnp.asarray(mall, dtype=BF16), jnp.asarray(lvl), nlev


def _gla_kernel(*refs, c, nlev, heads, dk, dv, sample_mode):
    if sample_mode:
        (q_ref, k_ref, v_ref, r_ref, z_ref, wg_ref, bg_ref, gn_ref, mall_ref, lvl_ref, s0_ref,
         og_ref, sout_ref, s_scr) = refs
    else:
        (q_ref, k_ref, v_ref, r_ref, z_ref, wg_ref, bg_ref, gn_ref, mall_ref, lvl_ref,
         og_ref, sout_ref, s_scr) = refs
    step = pl.program_id(1)

    if sample_mode:
        r0 = pl.program_id(0) % c
        row_mask = lax.broadcasted_iota(jnp.int32, (c, 1), 0) == r0
        s_scr[...] = s0_ref[0]
    else:
        @pl.when(step == 0)
        def _():
            s_scr[...] = jnp.zeros_like(s_scr)

    z = jnp.dot(z_ref[...], wg_ref[...], preferred_element_type=F32) + bg_ref[...]
    la_all = -(jnp.maximum(-z, 0.0) + jnp.log(1.0 + jnp.exp(-jnp.abs(z)))) / GATE_TEMP
    if sample_mode:
        la_all = jnp.where(row_mask, la_all, 0.0)
    lvl = lvl_ref[...]
    mall = mall_ref[...]
    gn = gn_ref[...]
    outs = []
    for hh in range(heads):
        la = la_all[:, hh * dk:(hh + 1) * dk]
        la_hi = la.astype(BF16)
        la_lo = (la - la_hi.astype(F32)).astype(BF16)
        e2 = jnp.dot(mall, jnp.concatenate([la_hi, la_lo], axis=1), preferred_element_type=F32)
        e = e2[:, :dk] + e2[:, dk:]
        b = e[0:c]
        q = q_ref[:, hh * dk:(hh + 1) * dk].astype(F32) * (dk ** -0.5)
        k = k_ref[:, hh * dk:(hh + 1) * dk].astype(F32)
        v = v_ref[:, hh * dv:(hh + 1) * dv]
        if sample_mode:
            k = jnp.where(row_mask, k, 0.0)
            v = jnp.where(row_mask, v, jnp.zeros_like(v))
        scores = jnp.where(lvl == 0, _dot_nt(q.astype(BF16), k.astype(BF16)), 0.0)
        for l in range(1, nlev + 1):
            eq = jnp.exp(e[2 * l * c:(2 * l + 1) * c])
            ek = jnp.exp(e[(2 * l + 1) * c:(2 * l + 2) * c])
            sc = _dot_nt((q * eq).astype(BF16), (k * ek).astype(BF16))
            scores = scores + jnp.where(lvl == l, sc, 0.0)
        s_prev = s_scr[hh]
        o = jnp.dot(scores.astype(BF16), v, preferred_element_type=F32)
        o = o + jnp.dot((q * jnp.exp(b)).astype(BF16), s_prev.astype(BF16), preferred_element_type=F32)
        dec = jnp.transpose(jnp.broadcast_to(jnp.exp(b[c - 1:c, :]), (dk, dk)))
        dec = jnp.concatenate([dec] * (dv // dk), axis=1)
        khat_t = jnp.transpose(k * jnp.exp(e[c:2 * c])).astype(BF16)
        s_scr[hh] = dec * s_prev + jnp.dot(khat_t, v, preferred_element_type=F32)
        og = _rmsnorm_rows(o, gn)
        rg = r_ref[:, hh * dv:(hh + 1) * dv].astype(F32)
        outs.append(og * (rg * _sigmoid(rg)))
    og_all = jnp.concatenate(outs, axis=1)
    if sample_mode:
        @pl.when(r0 == 0)
        def _():
            og_ref[...] = jnp.zeros_like(og_ref)
        og_ref[...] = jnp.where(row_mask, og_all, og_ref[...])
        sout_ref[0] = s_scr[...]
    else:
        og_ref[...] = og_all.astype(og_ref.dtype)

        @pl.when(step == pl.num_programs(1) - 1)
        def _():
            sout_ref[0] = s_scr[...]


def _gla(proj, wg, bg, gn, lay, heads, dk, dv, *, batch, seq, chunk, s0=None):
    sample_mode = s0 is not None
    t = proj.shape[0]
    mall, lvl, nlev = _gla_tables(chunk)
    q0, qw = lay["qb"]
    k0, kw = lay["kb"]
    v0, vw = lay["vb"]
    r0_, rw = lay["rb"]
    z0, zw = lay["z"]
    if sample_mode:
        grid = (t, 1)
        row = lambda b, s: b // chunk
    else:
        nchunk = seq // chunk
        grid = (batch, nchunk)
        row = lambda b, s: b * nchunk + s
    in_specs = [
        pl.BlockSpec((chunk, qw), lambda b, s: (row(b, s), q0 // qw)),
        pl.BlockSpec((chunk, kw), lambda b, s: (row(b, s), k0 // kw)),
        pl.BlockSpec((chunk, vw), lambda b, s: (row(b, s), v0 // vw)),
        pl.BlockSpec((chunk, rw), lambda b, s: (row(b, s), r0_ // rw)),
        pl.BlockSpec((chunk, zw), lambda b, s: (row(b, s), z0 // zw)),
        pl.BlockSpec(wg.shape, lambda b, s: (0, 0)),
        pl.BlockSpec((1, heads * dk), lambda b, s: (0, 0)),
        pl.BlockSpec((1, dv), lambda b, s: (0, 0)),
        pl.BlockSpec(mall.shape, lambda b, s: (0, 0)),
        pl.BlockSpec(lvl.shape, lambda b, s: (0, 0)),
    ]
    args = [proj, proj, proj, proj, proj, wg, bg.reshape(1, -1), gn.reshape(1, -1), mall, lvl]
    nstate = t if sample_mode else batch
    if sample_mode:
        in_specs.append(pl.BlockSpec((1, heads, dk, dv), lambda b, s: (b, 0, 0, 0)))
        args.append(s0)
    og_dtype = F32 if sample_mode else BF16
    est = (2 * chunk * (qw + kw + vw + rw + zw) * 2 + 2 * mall.size * 2 + 2 * lvl.size * 4
           + 5 * heads * dk * dv * 4 + 2 * chunk * vw * 4 + 8 * mall.shape[0] * 2 * dk * 4)
    return pl.pallas_call(
        functools.partial(_gla_kernel, c=chunk, nlev=nlev, heads=heads, dk=dk, dv=dv,
                          sample_mode=sample_mode),
        grid=grid,
        in_specs=in_specs,
        out_specs=[
            pl.BlockSpec((chunk, heads * dv), lambda b, s: (row(b, s), 0)),
            pl.BlockSpec((1, heads, dk, dv), lambda b, s: (b, 0, 0, 0)),
        ],
        out_shape=[
            jax.ShapeDtypeStruct((t, heads * dv), og_dtype),
            jax.ShapeDtypeStruct((nstate, heads, dk, dv), F32),
        ],
        scratch_shapes=[pltpu.VMEM((heads, dk, dv), F32)],
        compiler_params=pltpu.CompilerParams(
            dimension_semantics=("arbitrary", "arbitrary"), vmem_limit_bytes=_vmem_limit(est)),
        name="gla_sample" if sample_mode else "gla_prompt",
    )(*args)


def _post_kernel(x_ref, oa_ref, og_ref, ga_ref, gb_ref, wo_ref, g2_ref, wq_ref, h_ref, n2_ref, pq_ref):
    u = (_sigmoid(ga_ref[...].astype(F32)) * oa_ref[...].astype(F32)
         + _sigmoid(gb_ref[...].astype(F32)) * og_ref[...].astype(F32))
    h = x_ref[...] + jnp.dot(u.astype(BF16), wo_ref[...], preferred_element_type=F32)
    h_ref[...] = h
    n2 = _rmsnorm_rows(h, g2_ref[...]).astype(BF16)
    n2_ref[...] = n2
    pq_ref[...] = jnp.dot(n2, wq_ref[...], preferred_element_type=F32).astype(BF16)


def _post(x2d, oa, og, proj, wo, g2, wq, lay, tm):
    t, d = x2d.shape
    ga0, gaw = lay["ga"]
    gb0, gbw = lay["gb"]
    nq = wq.shape[1]
    est = (2 * tm * d * (4 + oa.dtype.itemsize + og.dtype.itemsize + 2 + 2 + 4 + 2)
           + 2 * d * d * 2 + 2 * d * nq * 2 + 2 * tm * nq * 2 + tm * nq * 4)
    return pl.pallas_call(
        _post_kernel,
        grid=(t // tm,),
        in_specs=[
            pl.BlockSpec((tm, d), lambda i: (i, 0)),
            pl.BlockSpec((tm, d), lambda i: (i, 0)),
            pl.BlockSpec((tm, d), lambda i: (i, 0)),
            pl.BlockSpec((tm, gaw), lambda i: (i, ga0 // gaw)),
            pl.BlockSpec((tm, gbw), lambda i: (i, gb0 // gbw)),
            pl.BlockSpec((d, d), lambda i: (0, 0)),
            pl.BlockSpec((1, d), lambda i: (0, 0)),
            pl.BlockSpec((d, nq), lambda i: (0, 0)),
        ],
        out_specs=[
            pl.BlockSpec((tm, d), lambda i: (i, 0)),
            pl.BlockSpec((tm, d), lambda i: (i, 0)),
            pl.BlockSpec((tm, nq), lambda i: (i, 0)),
        ],
        out_shape=[
            jax.ShapeDtypeStruct((t, d), F32),
            jax.ShapeDtypeStruct((t, d), BF16),
            jax.ShapeDtypeStruct((t, nq), BF16),
        ],
        compiler_params=pltpu.CompilerParams(
            dimension_semantics=("parallel",), vmem_limit_bytes=_vmem_limit(est)),
        name="post",
    )(x2d, oa, og, proj, proj, wo, g2.reshape(1, d), wq)


def _oddeven_mergesort_pairs(n):
    pairs = []
    p = 1
    while p < n:
        k = p
        while k >= 1:
            for j in range(k % p, n - k, 2 * k):
                for i in range(min(k, n - j - k)):
                    if (i + j) // (2 * p) == (i + j + k) // (2 * p):
                        pairs.append((i + j, i + j + k))
            k //= 2
        p *= 2
    return pairs


_SORT16_PAIRS = tuple(_oddeven_mergesort_pairs(PEER_TOPK))


def _cmpx(xs, i, j):
    a, b = xs[i], xs[j]
    if b is None:
        return
    if a is None:
        xs[i], xs[j] = b, None
        return
    xs[i], xs[j] = jnp.maximum(a, b), jnp.minimum(a, b)


def _bitonic_sort_desc(xs):
    n = len(xs)
    k = n // 2
    while k >= 1:
        for i in range(n):
            if i & k == 0:
                _cmpx(xs, i, i + k)
        k //= 2
    return xs


def _top_set(x, y):
    n = PEER_TOPK
    x = list(x) + [None] * (n - len(x))
    y = list(y) + [None] * (n - len(y))
    out = []
    for i in range(n):
        a, b = x[i], y[n - 1 - i]
        out.append(b if a is None else (a if b is None else jnp.maximum(a, b)))
    return out


def _merge_desc(x, y):
    return [v for v in _bitonic_sort_desc(_top_set(x, y)) if v is not None]


def _top16_of_keys(s_t):
    nslab = s_t.shape[0] // V7X_SUBLANES
    assert nslab == PEER_TOPK
    xs = [s_t[i * V7X_SUBLANES:(i + 1) * V7X_SUBLANES, :] for i in range(nslab)]
    for i, j in _SORT16_PAIRS:
        _cmpx(xs, i, j)
    shift = V7X_SUBLANES // 2
    while shift >= 1:
        ys = [pltpu.roll(xs[PEER_TOPK - 1 - i], shift, axis=0) for i in range(PEER_TOPK)]
        xs = _bitonic_sort_desc([jnp.maximum(a, b) for a, b in zip(xs, ys)])
        shift //= 2
    return xs


def _pair_sum_top_set(a, b):
    n = PEER_TOPK
    row = lambda i, j0, j1: [a[i] + b[j] for j in range(j0, j1)]
    col = lambda j, i0, i1: [a[i] + b[j] for i in range(i0, i1)]
    top_rows = _merge_desc(_merge_desc(row(0, 0, n), row(1, 0, n // 2)),
                           _merge_desc(row(2, 0, n // 3), row(3, 0, n // 4)))
    rest = _merge_desc(_merge_desc(col(0, 4, n), col(1, 4, n // 2)), col(2, 4, n // 3))
    return _top_set(top_rows, rest)


def _peer_kernel(n2_ref, pq_ref, h_ref, sk_ref, u_ref, vt_ref, fg_ref, y_ref,
                 n_scr, e1_scr, rank_scr, e2_scr, nt_scr, et_scr, w_scr, acc_scr,
                 *, heads, nkeys, tb, ec, final_norm):
    j = pl.program_id(1)
    ln = V7X_LANES
    ntile = tb // ln
    sub = lax.broadcasted_iota(jnp.int32, (V7X_SUBLANES, ln), 0)
    assert heads == V7X_SUBLANES

    @pl.when(j == 0)
    def _route():
        acc_scr[...] = jnp.zeros_like(acc_scr)

        def tile_body(lt, carry):
            l0 = pl.multiple_of(lt * ln, ln)
            a_pack = [None] * PEER_TOPK
            b_pack = [None] * PEER_TOPK
            for h in range(heads):
                for c in range(2):
                    col = (h * 2 + c) * nkeys
                    qt = pq_ref[pl.ds(l0, ln), col:col + nkeys]
                    s_t = _dot_nt(sk_ref[h * 2 + c], qt) * LOG2E
                    if c == 0:
                        n_scr[h, :, pl.ds(l0, ln)] = s_t
                    else:
                        e1_scr[h, :, pl.ds(l0, ln)] = s_t
                    top = _top16_of_keys(s_t)
                    pack = a_pack if c == 0 else b_pack
                    for r in range(PEER_TOPK):
                        pack[r] = top[r] if h == 0 else jnp.where(sub == h, top[r], pack[r])
            top_s = _pair_sum_top_set(a_pack, b_pack)
            m = a_pack[0] + b_pack[0]
            zsum = None
            for v in top_s:
                e = jnp.exp2(v - m)
                zsum = e if zsum is None else zsum + e
            shift = m + jnp.log2(zsum)
            d_pack = [a - shift for a in a_pack]
            top_u = _pair_sum_top_set(d_pack, b_pack)
            thr = top_u[0]
            for v in top_u[1:]:
                thr = jnp.minimum(thr, v)
            for h in range(heads):
                b_rows = [b_pack[r][h:h + 1, :] for r in range(PEER_TOPK)]
                d1 = n_scr[h, :, pl.ds(l0, ln)] - shift[h:h + 1, :]
                s2 = e1_scr[h, :, pl.ds(l0, ln)]
                thr_h = thr[h:h + 1, :]
                n = jnp.zeros((nkeys, ln), F32)
                rank = jnp.zeros((nkeys, ln), F32)
                for r in range(PEER_TOPK):
                    n = jnp.where(b_rows[r] + d1 >= thr_h, float(r + 1), n)
                    rank = jnp.where(s2 < b_rows[r], float(r + 1), rank)
                n_scr[h, :, pl.ds(l0, ln)] = n
                e1_scr[h, :, pl.ds(l0, ln)] = jnp.exp2(d1 + b_rows[0])
                rank_scr[h, :, pl.ds(l0, ln)] = rank.astype(BF16)
                e2_scr[h, :, pl.ds(l0, ln)] = jnp.exp2(s2 - b_rows[0]).astype(BF16)
            return carry

        lax.fori_loop(0, ntile, tile_body, 0)

    n_i1 = ec // nkeys
    assert n_i1 % V7X_SUBLANES == 0
    i1_base = pl.multiple_of(j * n_i1, V7X_SUBLANES)
    for h in range(heads):
        nt_scr[h] = n_scr[h, pl.ds(i1_base, n_i1), :]
        et_scr[h] = e1_scr[h, pl.ds(i1_base, n_i1), :]
    a_t = _dot_nt(u_ref[...], n2_ref[...])
    gl = (0.5 * a_t * (1.0 + lax.erf(a_t * INV_SQRT2))).astype(BF16)
    for il in range(n_i1):
        for lt in range(ntile):
            lanes = slice(lt * ln, (lt + 1) * ln)
            gate = jnp.zeros((nkeys, ln), BF16)
            for h in range(heads):
                n_row = nt_scr[h, il:il + 1, lanes].astype(BF16)
                e_row = et_scr[h, il:il + 1, lanes].astype(BF16)
                val = e2_scr[h, :, lanes] * e_row
                gate = gate + jnp.where(rank_scr[h, :, lanes] < n_row, val, jnp.zeros_like(val))
            w_scr[il * nkeys:(il + 1) * nkeys, lanes] = gate * gl[il * nkeys:(il + 1) * nkeys, lanes]
    acc_scr[...] += jnp.dot(vt_ref[...], w_scr[...], preferred_element_type=F32)

    @pl.when(j == pl.num_programs(1) - 1)
    def _finish():
        xo = h_ref[...] + jnp.transpose(acc_scr[...])
        y_ref[...] = _rmsnorm_rows(xo, fg_ref[...]) if final_norm else xo


def _peer(n2, pq, h, sk, u_bf, vt_bf, fg, *, heads, nkeys, tb, ec, final_norm):
    t, d = h.shape
    ne = u_bf.shape[0]
    nq = pq.shape[1]
    n_i1 = ec // nkeys
    nchunk = ne // ec
    est = (2 * tb * d * 2 + 2 * tb * nq * 2 + 2 * tb * d * 4 + 2 * sk.size * 2 + 4 * ec * d * 2
           + 2 * tb * d * 4 + heads * nkeys * tb * (4 + 4 + 2 + 2) + 2 * heads * n_i1 * tb * 4
           + 2 * ec * tb * 2 + d * tb * 4 + 3 * ec * tb * 4)
    return pl.pallas_call(
        functools.partial(_peer_kernel, heads=heads, nkeys=nkeys, tb=tb, ec=ec, final_norm=final_norm),
        grid=(t // tb, nchunk),
        in_specs=[
            pl.BlockSpec((tb, d), lambda i, j: (i, 0)),
            pl.BlockSpec((tb, nq), lambda i, j: (i, 0)),
            pl.BlockSpec((tb, d), lambda i, j: (i, 0)),
            pl.BlockSpec(sk.shape, lambda i, j: (0, 0, 0)),
            pl.BlockSpec((ec, d), lambda i, j: (j, 0)),
            pl.BlockSpec((d, ec), lambda i, j: (0, j)),
            pl.BlockSpec((1, d), lambda i, j: (0, 0)),
        ],
        out_specs=pl.BlockSpec((tb, d), lambda i, j: (i, 0)),
        out_shape=jax.ShapeDtypeStruct((t, d), F32),
        scratch_shapes=[
            pltpu.VMEM((heads, nkeys, tb), F32),
            pltpu.VMEM((heads, nkeys, tb), F32),
            pltpu.VMEM((heads, nkeys, tb), BF16),
            pltpu.VMEM((heads, nkeys, tb), BF16),
            pltpu.VMEM((heads, n_i1, tb), F32),
            pltpu.VMEM((heads, n_i1, tb), F32),
            pltpu.VMEM((ec, tb), BF16),
            pltpu.VMEM((d, tb), F32),
        ],
        compiler_params=pltpu.CompilerParams(
            dimension_semantics=("parallel", "arbitrary"), vmem_limit_bytes=_vmem_limit(est)),
        name="peer",
    )(n2, pq, h, sk, u_bf, vt_bf, fg.reshape(1, d))


def _layout(d, heads_a, kv_a, hd_a, heads_b, dk, dv, rank):
    widths = [("qa", heads_a * hd_a), ("vb", heads_b * dv), ("rb", heads_b * dv), ("ga", d), ("gb", d),
              ("qb", heads_b * dk), ("kb", heads_b * dk), ("kv", 2 * kv_a * hd_a), ("z", V7X_LANES)]
    lay, off = {}, 0
    for name, w in widths:
        assert off % w == 0
        lay[name] = (off, w)
        off += w
    assert rank <= V7X_LANES
    lay["ncol"] = off
    return lay


def _prep_w_in(w, cols, rank):
    qa, ka, va, qb, kb, vb, rb, zb, ga, gb = jnp.split(w, np.cumsum(cols)[:-1].tolist(), axis=1)
    zpad = jnp.pad(zb, ((0, 0), (0, V7X_LANES - rank)))
    return jnp.concatenate([qa, vb, rb, ga, gb, qb, kb, ka, va, zpad], axis=1).astype(BF16)


def kernel(x_prompt, x_sample, cache_k_win, cache_v_win, state_gla, norm1_g, w_in, w_gla_gate, b_gla_gate,
           attn_sinks, gla_norm_g, w_out, norm2_g, peer_wq, peer_subkeys, peer_u, peer_v, final_norm_g):
    batch, seq, d = x_prompt.shape
    nsamp, dec_seq, _ = x_sample.shape
    assert dec_seq == 1
    depth, _, window, kv_a, hd_a = cache_k_win.shape
    heads_a = attn_sinks.shape[1]
    group = heads_a // kv_a
    _, _, heads_b, dk, dv = state_gla.shape
    rank = w_gla_gate.shape[1]
    peer_heads, _, nkeys, dhalf = peer_subkeys.shape[1:]
    n_exp = peer_u.shape[1]
    assert n_exp == nkeys * nkeys and dhalf == nkeys == V7X_LANES
    cols = (heads_a * hd_a, kv_a * hd_a, kv_a * hd_a, heads_b * dk, heads_b * dk, heads_b * dv,
            heads_b * dv, rank, d, d)
    lay = _layout(d, heads_a, kv_a, hd_a, heads_b, dk, dv, rank)
    kvd = kv_a * hd_a

    yp = x_prompt.reshape(batch * seq, d)
    ys = x_sample.reshape(nsamp, d)
    outs = {k: [] for k in ("kp", "vp", "sp", "ks", "vs", "ss")}
    for l in range(depth):
        w_l = _prep_w_in(w_in[l], cols, rank)
        wg = jnp.pad(w_gla_gate[l], ((0, V7X_LANES - rank), (0, 0))).astype(BF16)
        wo = w_out[l].astype(BF16)
        wq = peer_wq[l].astype(BF16)
        sk = peer_subkeys[l].reshape(peer_heads * 2, nkeys, dhalf).astype(BF16)
        u_bf = peer_u[l].astype(BF16)
        vt_bf = jnp.transpose(peer_v[l]).astype(BF16)
        last = l == depth - 1

        proj_p, kv_p = _inproj(yp, norm1_g[l], w_l, lay, tm=512)
        oa_p = _swa_prompt(proj_p, kv_p, attn_sinks[l], lay, batch, seq, heads_a, kv_a, hd_a, window)
        og_p, s_p = _gla(proj_p, wg, b_gla_gate[l], gla_norm_g[l], lay, heads_b, dk, dv,
                         batch=batch, seq=seq, chunk=128)
        h_p, n2_p, pq_p = _post(yp, oa_p, og_p, proj_p, wo, norm2_g[l], wq, lay, tm=512)
        yp = _peer(n2_p, pq_p, h_p, sk, u_bf, vt_bf, final_norm_g, heads=peer_heads, nkeys=nkeys,
                   tb=512, ec=2 * V7X_SUBLANES * nkeys, final_norm=last)
        kv_last = kv_p.reshape(batch, seq, 2 * kvd)[:, seq - window:, :]
        outs["kp"].append(kv_last[..., :kvd].reshape(batch, window, kv_a, hd_a))
        outs["vp"].append(kv_last[..., kvd:].reshape(batch, window, kv_a, hd_a))
        outs["sp"].append(s_p)

        proj_s, kv_s = _inproj(ys, norm1_g[l], w_l, lay, tm=nsamp)
        qa_s = proj_s[:, :heads_a * hd_a].reshape(nsamp, kv_a, group, hd_a)
        qrep = jnp.transpose(qa_s, (0, 2, 1, 3)).reshape(nsamp, group, 1, kvd)
        qrep = jnp.broadcast_to(qrep, (nsamp, group, kv_a, kvd)).reshape(nsamp, heads_a, kvd)
        row_head = np.array([(r % kv_a) * group + r // kv_a for r in range(heads_a)])
        sink_col = attn_sinks[l][row_head].reshape(heads_a, 1)
        slope_col = jnp.asarray(2.0 ** (-8.0 * (row_head + 1) / heads_a), F32).reshape(heads_a, 1)
        nk_s, nv_s, o_rows = _swa_sample(
            qrep, cache_k_win[l].reshape(nsamp, window, kvd), cache_v_win[l].reshape(nsamp, window, kvd),
            kv_s, sink_col, slope_col, kv_a, hd_a, window)
        oa_s = o_rows[:, ::kv_a, :].reshape(nsamp, group, kv_a, hd_a)
        oa_s = jnp.transpose(oa_s, (0, 2, 1, 3)).reshape(nsamp, d)
        og_s, s_s = _gla(proj_s, wg, b_gla_gate[l], gla_norm_g[l], lay, heads_b, dk, dv,
                         batch=nsamp, seq=1, chunk=V7X_SUBLANES, s0=state_gla[l])
        h_s, n2_s, pq_s = _post(ys, oa_s, og_s, proj_s, wo, norm2_g[l], wq, lay, tm=nsamp)
        ys = _peer(n2_s, pq_s, h_s, sk, u_bf, vt_bf, final_norm_g, heads=peer_heads, nkeys=nkeys,
                   tb=nsamp, ec=V7X_SUBLANES * nkeys, final_norm=last)
        outs["ks"].append(nk_s.reshape(nsamp, window, kv_a, hd_a))
        outs["vs"].append(nv_s.reshape(nsamp, window, kv_a, hd_a))
        outs["ss"].append(s_s)

    return (yp.reshape(batch, seq, d), ys.reshape(nsamp, 1, d),
            jnp.stack(outs["kp"]), jnp.stack(outs["vp"]), jnp.stack(outs["sp"]),
            jnp.stack(outs["ks"]), jnp.stack(outs["vs"]), jnp.stack(outs["ss"]))
```

```python
import functools
import math

import numpy as np
import jax
import jax.numpy as jnp
from jax import lax
from jax.experimental import pallas as pl
from jax.experimental.pallas import tpu as pltpu

F32 = jnp.float32
BF16 = jnp.bfloat16

EPS = 1e-6
GATE_TEMP = 16.0
PEER_TOPK = 16
LOG2E = 1.4426950408889634
INV_SQRT2 = 0.7071067811865476

V7X_LANES = 128
V7X_SUBLANES = 8
V7X_MXU_DIM = 256
V7X_VMEM_BYTES = 64 * 1024 * 1024
MIB = 1024 * 1024


def _vmem_limit(estimate_bytes):
    return int(min(estimate_bytes + 16 * MIB, V7X_VMEM_BYTES - 8 * MIB))


def _dot_nt(a, b):
    return lax.dot_general(a, b, (((1,), (1,)), ((), ())), preferred_element_type=F32)


def _sigmoid(z):
    return 1.0 / (1.0 + jnp.exp(-z))


def _rmsnorm_rows(x, g):
    ms = jnp.mean(x * x, axis=-1, keepdims=True)
    return x * lax.rsqrt(ms + EPS) * g


def _inproj_kernel(x_ref, g_ref, w_ref, proj_ref, kv_ref, *, chunks, kv_chunk):
    n = _rmsnorm_rows(x_ref[...], g_ref[...]).astype(BF16)
    for c0, c1 in chunks:
        acc = jnp.dot(n, w_ref[:, c0:c1], preferred_element_type=F32)
        proj_ref[:, c0:c1] = acc.astype(BF16)
        if (c0, c1) == kv_chunk:
            kv_ref[...] = acc


def _inproj(x2d, g, w, lay, tm):
    t, d = x2d.shape
    ncol = w.shape[1]
    kv0, kvw = lay["kv"]
    step = 512
    chunks = tuple((c, min(c + step, ncol)) for c in range(0, ncol, step))
    assert (kv0, kv0 + kvw) in chunks
    est = 2 * tm * d * 4 + 2 * d * ncol * 2 + 2 * tm * ncol * 2 + 2 * tm * kvw * 4
    return pl.pallas_call(
        functools.partial(_inproj_kernel, chunks=chunks, kv_chunk=(kv0, kv0 + kvw)),
        grid=(t // tm,),
        in_specs=[
            pl.BlockSpec((tm, d), lambda i: (i, 0)),
            pl.BlockSpec((1, d), lambda i: (0, 0)),
            pl.BlockSpec((d, ncol), lambda i: (0, 0)),
        ],
        out_specs=[
            pl.BlockSpec((tm, ncol), lambda i: (i, 0)),
            pl.BlockSpec((tm, kvw), lambda i: (i, 0)),
        ],
        out_shape=[
            jax.ShapeDtypeStruct((t, ncol), BF16),
            jax.ShapeDtypeStruct((t, kvw), F32),
        ],
        compiler_params=pltpu.CompilerParams(
            dimension_semantics=("parallel",), vmem_limit_bytes=_vmem_limit(est)),
        name="inproj",
    )(x2d, g.reshape(1, d), w)


def _swa_prompt_kernel(sink_ref, q_ref, kvc_ref, kvp_ref, o_ref, *, heads, kv_heads, hd, window, slopes):
    j = pl.program_id(1)
    group = heads // kv_heads
    kvd = kv_heads * hd
    kvc = kvc_ref[...]
    kvp = kvp_ref[...]
    k_all = jnp.concatenate([kvp[:, :kvd], kvc[:, :kvd]], axis=0).astype(BF16)
    v_all = jnp.concatenate([kvp[:, kvd:], kvc[:, kvd:]], axis=0).astype(BF16)
    t_idx = lax.broadcasted_iota(jnp.int32, (window, 2 * window), 0)
    s_idx = lax.broadcasted_iota(jnp.int32, (window, 2 * window), 1)
    dist = window + t_idx - s_idx
    valid = (dist >= 0) & (dist < window) & ((s_idx >= window) | (j > 0))
    distf = dist.astype(F32)
    scale = hd ** -0.5
    for h in range(heads):
        kvh = h // group
        qh = q_ref[:, h * hd:(h + 1) * hd]
        kh = k_all[:, kvh * hd:(kvh + 1) * hd]
        vh = v_all[:, kvh * hd:(kvh + 1) * hd]
        s = _dot_nt(qh, kh) * scale - slopes[h] * distf
        s = jnp.where(valid, s, -jnp.inf)
        sink = sink_ref[h]
        m = jnp.maximum(jnp.max(s, axis=-1, keepdims=True), sink)
        p = jnp.exp(s - m)
        denom = jnp.sum(p, axis=-1, keepdims=True) + jnp.exp(sink - m)
        oh = jnp.dot((p / denom).astype(BF16), vh, preferred_element_type=F32)
        o_ref[:, h * hd:(h + 1) * hd] = oh.astype(o_ref.dtype)


def _swa_prompt(proj, kv, sinks, lay, batch, seq, heads, kv_heads, hd, window):
    t = batch * seq
    nb = seq // window
    q0, qw = lay["qa"]
    assert q0 == 0
    kvw = kv.shape[1]
    slopes = tuple(float(2.0 ** (-8.0 * (h + 1) / heads)) for h in range(heads))
    est = 2 * window * qw * 2 * 2 + 4 * window * kvw * 4
    return pl.pallas_call(
        functools.partial(_swa_prompt_kernel, heads=heads, kv_heads=kv_heads, hd=hd,
                          window=window, slopes=slopes),
        grid=(batch, nb),
        in_specs=[
            pl.BlockSpec(memory_space=pltpu.SMEM),
            pl.BlockSpec((window, qw), lambda b, j: (b * nb + j, 0)),
            pl.BlockSpec((window, kvw), lambda b, j: (b * nb + j, 0)),
            pl.BlockSpec((window, kvw), lambda b, j: (b * nb + jnp.maximum(j - 1, 0), 0)),
        ],
        out_specs=pl.BlockSpec((window, qw), lambda b, j: (b * nb + j, 0)),
        out_shape=jax.ShapeDtypeStruct((t, qw), BF16),
        compiler_params=pltpu.CompilerParams(
            dimension_semantics=("parallel", "parallel"), vmem_limit_bytes=_vmem_limit(est)),
        name="swa_prompt",
    )(sinks, proj, kv, kv)


def _swa_sample_kernel(qrep_ref, ck_ref, cv_ref, kvn_ref, sink_ref, slope_ref, nk_ref, nv_ref, o_ref,
                       *, bs, kv_heads, hd, window):
    kvd = kv_heads * hd
    rows = qrep_ref.shape[1]
    row_idx = lax.broadcasted_iota(jnp.int32, (window, kvd), 0)
    lane_blk = lax.broadcasted_iota(jnp.int32, (rows, kvd), 1) // hd
    row_kv = lax.broadcasted_iota(jnp.int32, (rows, kvd), 0) % kv_heads
    head_mask = lane_blk == row_kv
    dist = (window - 1 - lax.broadcasted_iota(jnp.int32, (rows, window), 1)).astype(F32)
    sink = sink_ref[...]
    slope = slope_ref[...]
    scale = hd ** -0.5
    for i in range(bs):
        knew = kvn_ref[i:i + 1, :kvd]
        vnew = kvn_ref[i:i + 1, kvd:]
        nk = jnp.where(row_idx == window - 1, knew, pltpu.roll(ck_ref[i], window - 1, axis=0))
        nv = jnp.where(row_idx == window - 1, vnew, pltpu.roll(cv_ref[i], window - 1, axis=0))
        nk_ref[i] = nk
        nv_ref[i] = nv
        qe = jnp.where(head_mask, qrep_ref[i], jnp.zeros_like(qrep_ref[i]))
        s = _dot_nt(qe, nk.astype(BF16)) * scale - slope * dist
        m = jnp.maximum(jnp.max(s, axis=-1, keepdims=True), sink)
        p = jnp.exp(s - m)
        denom = jnp.sum(p, axis=-1, keepdims=True) + jnp.exp(sink - m)
        r = jnp.dot((p / denom).astype(BF16), nv.astype(BF16), preferred_element_type=F32)
        r = jnp.where(head_mask, r, 0.0)
        shift = 1
        while shift < kv_heads:
            r = r + pltpu.roll(r, rows - shift, axis=0)
            shift *= 2
        o_ref[i] = r


def _swa_sample(qrep, ck, cv, kvn, sink_col, slope_col, kv_heads, hd, window, bs=8):
    s, rows, kvd = qrep.shape
    est = 2 * bs * (rows * kvd * 2 + 4 * window * kvd * 4 + 2 * kvd * 4 + rows * kvd * 4)
    return pl.pallas_call(
        functools.partial(_swa_sample_kernel, bs=bs, kv_heads=kv_heads, hd=hd, window=window),
        grid=(s // bs,),
        in_specs=[
            pl.BlockSpec((bs, rows, kvd), lambda i: (i, 0, 0)),
            pl.BlockSpec((bs, window, kvd), lambda i: (i, 0, 0)),
            pl.BlockSpec((bs, window, kvd), lambda i: (i, 0, 0)),
            pl.BlockSpec((bs, 2 * kvd), lambda i: (i, 0)),
            pl.BlockSpec((rows, 1), lambda i: (0, 0)),
            pl.BlockSpec((rows, 1), lambda i: (0, 0)),
        ],
        out_specs=[
            pl.BlockSpec((bs, window, kvd), lambda i: (i, 0, 0)),
            pl.BlockSpec((bs, window, kvd), lambda i: (i, 0, 0)),
            pl.BlockSpec((bs, rows, kvd), lambda i: (i, 0, 0)),
        ],
        out_shape=[
            jax.ShapeDtypeStruct((s, window, kvd), F32),
            jax.ShapeDtypeStruct((s, window, kvd), F32),
            jax.ShapeDtypeStruct((s, rows, kvd), F32),
        ],
        compiler_params=pltpu.CompilerParams(
            dimension_semantics=("parallel",), vmem_limit_bytes=_vmem_limit(est)),
        name="swa_sample",
    )(qrep, ck, cv, kvn, sink_col, slope_col)


def _gla_tables(c):
    nlev = int(round(math.log2(c)))
    assert 1 << nlev == c
    t = np.arange(c)
    u = t[None, :]
    rows = [u <= t[:, None], u > t[:, None]]
    lvl = np.full((c, c), -1, np.int32)
    lvl[t, t] = 0
    for l in range(1, nlev + 1):
        hs = 1 << (l - 1)
        pos = t % (2 * hs)
        right = pos >= hs
        bnd = (t - pos + hs - 1)[:, None]
        rows.append(right[:, None] & (u > bnd) & (u <= t[:, None]))
        rows.append((~right)[:, None] & (u > t[:, None]) & (u <= bnd))
        same = (t[:, None] // (2 * hs)) == (t[None, :] // (2 * hs))
        lvl[same & right[:, None] & (~right)[None, :]] = l
    mall = np.concatenate(rows, axis=0).astype(np.float32)
    return jnp.asarray(mall, dtype=BF16), jnp.asarray(lvl), nlev


def _gla_kernel(*refs, c, nlev, heads, dk, dv, sample_mode):
    if sample_mode:
        (q_ref, k_ref, v_ref, r_ref, z_ref, wg_ref, bg_ref, gn_ref, mall_ref, lvl_ref, s0_ref,
         og_ref, sout_ref, s_scr) = refs
    else:
        (q_ref, k_ref, v_ref, r_ref, z_ref, wg_ref, bg_ref, gn_ref, mall_ref, lvl_ref,
         og_ref, sout_ref, s_scr) = refs
    step = pl.program_id(1)

    if sample_mode:
        r0 = pl.program_id(0) % c
        row_mask = lax.broadcasted_iota(jnp.int32, (c, 1), 0) == r0
        s_scr[...] = s0_ref[0]
    else:
        @pl.when(step == 0)
        def _():
            s_scr[...] = jnp.zeros_like(s_scr)

    z = jnp.dot(z_ref[...], wg_ref[...], preferred_element_type=F32) + bg_ref[...]
    la_all = -(jnp.maximum(-z, 0.0) + jnp.log(1.0 + jnp.exp(-jnp.abs(z)))) / GATE_TEMP
    if sample_mode:
        la_all = jnp.where(row_mask, la_all, 0.0)
    lvl = lvl_ref[...]
    mall = mall_ref[...]
    gn = gn_ref[...]
    outs = []
    for hh in range(heads):
        la = la_all[:, hh * dk:(hh + 1) * dk]
        la_hi = la.astype(BF16)
        la_lo = (la - la_hi.astype(F32)).astype(BF16)
        e2 = jnp.dot(mall, jnp.concatenate([la_hi, la_lo], axis=1), preferred_element_type=F32)
        e = e2[:, :dk] + e2[:, dk:]
        b = e[0:c]
        q = q_ref[:, hh * dk:(hh + 1) * dk].astype(F32) * (dk ** -0.5)
        k = k_ref[:, hh * dk:(hh + 1) * dk].astype(F32)
        v = v_ref[:, hh * dv:(hh + 1) * dv]
        if sample_mode:
            k = jnp.where(row_mask, k, 0.0)
            v = jnp.where(row_mask, v, jnp.zeros_like(v))
        scores = jnp.where(lvl == 0, _dot_nt(q.astype(BF16), k.astype(BF16)), 0.0)
        for l in range(1, nlev + 1):
            eq = jnp.exp(e[2 * l * c:(2 * l + 1) * c])
            ek = jnp.exp(e[(2 * l + 1) * c:(2 * l + 2) * c])
            sc = _dot_nt((q * eq).astype(BF16), (k * ek).astype(BF16))
            scores = scores + jnp.where(lvl == l, sc, 0.0)
        s_prev = s_scr[hh]
        o = jnp.dot(scores.astype(BF16), v, preferred_element_type=F32)
        o = o + jnp.dot((q * jnp.exp(b)).astype(BF16), s_prev.astype(BF16), preferred_element_type=F32)
        dec = jnp.transpose(jnp.broadcast_to(jnp.exp(b[c - 1:c, :]), (dk, dk)))
        dec = jnp.concatenate([dec] * (dv // dk), axis=1)
        khat_t = jnp.transpose(k * jnp.exp(e[c:2 * c])).astype(BF16)
        s_scr[hh] = dec * s_prev + jnp.dot(khat_t, v, preferred_element_type=F32)
        og = _rmsnorm_rows(o, gn)
        rg = r_ref[:, hh * dv:(hh + 1) * dv].astype(F32)
        outs.append(og * (rg * _sigmoid(rg)))
    og_all = jnp.concatenate(outs, axis=1)
    if sample_mode:
        @pl.when(r0 == 0)
        def _():
            og_ref[...] = jnp.zeros_like(og_ref)
        og_ref[...] = jnp.where(row_mask, og_all, og_ref[...])
        sout_ref[0] = s_scr[...]
    else:
        og_ref[...] = og_all.astype(og_ref.dtype)

        @pl.when(step == pl.num_programs(1) - 1)
        def _():
            sout_ref[0] = s_scr[...]


def _gla(proj, wg, bg, gn, lay, heads, dk, dv, *, batch, seq, chunk, s0=None):
    sample_mode = s0 is not None
    t = proj.shape[0]
    mall, lvl, nlev = _gla_tables(chunk)
    q0, qw = lay["qb"]
    k0, kw = lay["kb"]
    v0, vw = lay["vb"]
    r0_, rw = lay["rb"]
    z0, zw = lay["z"]
    if sample_mode:
        grid = (t, 1)
        row = lambda b, s: b // chunk
    else:
        nchunk = seq // chunk
        grid = (batch, nchunk)
        row = lambda b, s: b * nchunk + s
    in_specs = [
        pl.BlockSpec((chunk, qw), lambda b, s: (row(b, s), q0 // qw)),
        pl.BlockSpec((chunk, kw), lambda b, s: (row(b, s), k0 // kw)),
        pl.BlockSpec((chunk, vw), lambda b, s: (row(b, s), v0 // vw)),
        pl.BlockSpec((chunk, rw), lambda b, s: (row(b, s), r0_ // rw)),
        pl.BlockSpec((chunk, zw), lambda b, s: (row(b, s), z0 // zw)),
        pl.BlockSpec(wg.shape, lambda b, s: (0, 0)),
        pl.BlockSpec((1, heads * dk), lambda b, s: (0, 0)),
        pl.BlockSpec((1, dv), lambda b, s: (0, 0)),
        pl.BlockSpec(mall.shape, lambda b, s: (0, 0)),
        pl.BlockSpec(lvl.shape, lambda b, s: (0, 0)),
    ]
    args = [proj, proj, proj, proj, proj, wg, bg.reshape(1, -1), gn.reshape(1, -1), mall, lvl]
    nstate = t if sample_mode else batch
    if sample_mode:
        in_specs.append(pl.BlockSpec((1, heads, dk, dv), lambda b, s: (b, 0, 0, 0)))
        args.append(s0)
    og_dtype = F32 if sample_mode else BF16
    est = (2 * chunk * (qw + kw + vw + rw + zw) * 2 + 2 * mall.size * 2 + 2 * lvl.size * 4
           + 5 * heads * dk * dv * 4 + 2 * chunk * vw * 4 + 8 * mall.shape[0] * 2 * dk * 4)
    return pl.pallas_call(
        functools.partial(_gla_kernel, c=chunk, nlev=nlev, heads=heads, dk=dk, dv=dv,
                          sample_mode=sample_mode),
        grid=grid,
        in_specs=in_specs,
        out_specs=[
            pl.BlockSpec((chunk, heads * dv), lambda b, s: (row(b, s), 0)),
            pl.BlockSpec((1, heads, dk, dv), lambda b, s: (b, 0, 0, 0)),
        ],
        out_shape=[
            jax.ShapeDtypeStruct((t, heads * dv), og_dtype),
            jax.ShapeDtypeStruct((nstate, heads, dk, dv), F32),
        ],
        scratch_shapes=[pltpu.VMEM((heads, dk, dv), F32)],
        compiler_params=pltpu.CompilerParams(
            dimension_semantics=("arbitrary", "arbitrary"), vmem_limit_bytes=_vmem_limit(est)),
        name="gla_sample" if sample_mode else "gla_prompt",
    )(*args)


def _post_kernel(x_ref, oa_ref, og_ref, ga_ref, gb_ref, wo_ref, g2_ref, wq_ref, h_ref, n2_ref, pq_ref):
    u = (_sigmoid(ga_ref[...].astype(F32)) * oa_ref[...].astype(F32)
         + _sigmoid(gb_ref[...].astype(F32)) * og_ref[...].astype(F32))
    h = x_ref[...] + jnp.dot(u.astype(BF16), wo_ref[...], preferred_element_type=F32)
    h_ref[...] = h
    n2 = _rmsnorm_rows(h, g2_ref[...]).astype(BF16)
    n2_ref[...] = n2
    pq_ref[...] = jnp.dot(n2, wq_ref[...], preferred_element_type=F32).astype(BF16)


def _post(x2d, oa, og, proj, wo, g2, wq, lay, tm):
    t, d = x2d.shape
    ga0, gaw = lay["ga"]
    gb0, gbw = lay["gb"]
    nq = wq.shape[1]
    est = (2 * tm * d * (4 + oa.dtype.itemsize + og.dtype.itemsize + 2 + 2 + 4 + 2)
           + 2 * d * d * 2 + 2 * d * nq * 2 + 2 * tm * nq * 2 + tm * nq * 4)
    return pl.pallas_call(
        _post_kernel,
        grid=(t // tm,),
        in_specs=[
            pl.BlockSpec((tm, d), lambda i: (i, 0)),
            pl.BlockSpec((tm, d), lambda i: (i, 0)),
            pl.BlockSpec((tm, d), lambda i: (i, 0)),
            pl.BlockSpec((tm, gaw), lambda i: (i, ga0 // gaw)),
            pl.BlockSpec((tm, gbw), lambda i: (i, gb0 // gbw)),
            pl.BlockSpec((d, d), lambda i: (0, 0)),
            pl.BlockSpec((1, d), lambda i: (0, 0)),
            pl.BlockSpec((d, nq), lambda i: (0, 0)),
        ],
        out_specs=[
            pl.BlockSpec((tm, d), lambda i: (i, 0)),
            pl.BlockSpec((tm, d), lambda i: (i, 0)),
            pl.BlockSpec((tm, nq), lambda i: (i, 0)),
        ],
        out_shape=[
            jax.ShapeDtypeStruct((t, d), F32),
            jax.ShapeDtypeStruct((t, d), BF16),
            jax.ShapeDtypeStruct((t, nq), BF16),
        ],
        compiler_params=pltpu.CompilerParams(
            dimension_semantics=("parallel",), vmem_limit_bytes=_vmem_limit(est)),
        name="post",
    )(x2d, oa, og, proj, proj, wo, g2.reshape(1, d), wq)


def _oddeven_mergesort_pairs(n):
    pairs = []
    p = 1
    while p < n:
        k = p
        while k >= 1:
            for j in range(k % p, n - k, 2 * k):
                for i in range(min(k, n - j - k)):
                    if (i + j) // (2 * p) == (i + j + k) // (2 * p):
                        pairs.append((i + j, i + j + k))
            k //= 2
        p *= 2
    return pairs


_SORT16_PAIRS = tuple(_oddeven_mergesort_pairs(PEER_TOPK))


def _cmpx(xs, i, j):
    a, b = xs[i], xs[j]
    if b is None:
        return
    if a is None:
        xs[i], xs[j] = b, None
        return
    xs[i], xs[j] = jnp.maximum(a, b), jnp.minimum(a, b)


def _bitonic_sort_desc(xs):
    n = len(xs)
    k = n // 2
    while k >= 1:
        for i in range(n):
            if i & k == 0:
                _cmpx(xs, i, i + k)
        k //= 2
    return xs


def _top_set(x, y):
    n = PEER_TOPK
    x = list(x) + [None] * (n - len(x))
    y = list(y) + [None] * (n - len(y))
    out = []
    for i in range(n):
        a, b = x[i], y[n - 1 - i]
        out.append(b if a is None else (a if b is None else jnp.maximum(a, b)))
    return out


def _merge_desc(x, y):
    return [v for v in _bitonic_sort_desc(_top_set(x, y)) if v is not None]


def _top16_of_keys(s_t):
    nslab = s_t.shape[0] // V7X_SUBLANES
    assert nslab == PEER_TOPK
    xs = [s_t[i * V7X_SUBLANES:(i + 1) * V7X_SUBLANES, :] for i in range(nslab)]
    for i, j in _SORT16_PAIRS:
        _cmpx(xs, i, j)
    shift = V7X_SUBLANES // 2
    while shift >= 1:
        ys = [pltpu.roll(xs[PEER_TOPK - 1 - i], shift, axis=0) for i in range(PEER_TOPK)]
        xs = _bitonic_sort_desc([jnp.maximum(a, b) for a, b in zip(xs, ys)])
        shift //= 2
    return xs


def _pair_sum_top_set(a, b):
    n = PEER_TOPK
    row = lambda i, j0, j1: [a[i] + b[j] for j in range(j0, j1)]
    col = lambda j, i0, i1: [a[i] + b[j] for i in range(i0, i1)]
    top_rows = _merge_desc(_merge_desc(row(0, 0, n), row(1, 0, n // 2)),
                           _merge_desc(row(2, 0, n // 3), row(3, 0, n // 4)))
    rest = _merge_desc(_merge_desc(col(0, 4, n), col(1, 4, n // 2)), col(2, 4, n // 3))
    return _top_set(top_rows, rest)


def _peer_kernel(n2_ref, pq_ref, h_ref, sk_ref, u_ref, vt_ref, fg_ref, y_ref,
                 n_scr, e1_scr, rank_scr, e2_scr, nt_scr, et_scr, g_scr, acc_scr,
                 *, heads, nkeys, tb, ec, final_norm):
    j = pl.program_id(1)
    ln = V7X_LANES
    ntile = tb // ln
    sub = lax.broadcasted_iota(jnp.int32, (V7X_SUBLANES, ln), 0)
    assert heads == V7X_SUBLANES

    @pl.when(j == 0)
    def _route():
        acc_scr[...] = jnp.zeros_like(acc_scr)

        def tile_body(lt, carry):
            l0 = pl.multiple_of(lt * ln, ln)
            a_pack = [None] * PEER_TOPK
            b_pack = [None] * PEER_TOPK
            for h in range(heads):
                for c in range(2):
                    col = (h * 2 + c) * nkeys
                    qt = pq_ref[pl.ds(l0, ln), col:col + nkeys]
                    s_t = _dot_nt(sk_ref[h * 2 + c], qt) * LOG2E
                    if c == 0:
                        n_scr[h, :, pl.ds(l0, ln)] = s_t
                    else:
                        e1_scr[h, :, pl.ds(l0, ln)] = s_t
                    top = _top16_of_keys(s_t)
                    pack = a_pack if c == 0 else b_pack
                    for r in range(PEER_TOPK):
                        pack[r] = top[r] if h == 0 else jnp.where(sub == h, top[r], pack[r])
            top_s = _pair_sum_top_set(a_pack, b_pack)
            m = a_pack[0] + b_pack[0]
            zsum = None
            for v in top_s:
                e = jnp.exp2(v - m)
                zsum = e if zsum is None else zsum + e
            shift = m + jnp.log2(zsum)
            d_pack = [a - shift for a in a_pack]
            top_u = _pair_sum_top_set(d_pack, b_pack)
            thr = top_u[0]
            for v in top_u[1:]:
                thr = jnp.minimum(thr, v)
            for h in range(heads):
                b_rows = [b_pack[r][h:h + 1, :] for r in range(PEER_TOPK)]
                d1 = n_scr[h, :, pl.ds(l0, ln)] - shift[h:h + 1, :]
                s2 = e1_scr[h, :, pl.ds(l0, ln)]
                thr_h = thr[h:h + 1, :]
                n = jnp.zeros((nkeys, ln), F32)
                rank = jnp.zeros((nkeys, ln), F32)
                for r in range(PEER_TOPK):
                    n = jnp.where(b_rows[r] + d1 >= thr_h, float(r + 1), n)
                    rank = jnp.where(s2 < b_rows[r], float(r + 1), rank)
                n_scr[h, :, pl.ds(l0, ln)] = n
                e1_scr[h, :, pl.ds(l0, ln)] = jnp.exp2(d1 + b_rows[0])
                rank_scr[h, :, pl.ds(l0, ln)] = rank.astype(rank_scr.dtype)
                e2_scr[h, :, pl.ds(l0, ln)] = jnp.exp2(s2 - b_rows[0]).astype(e2_scr.dtype)
            return carry

        lax.fori_loop(0, ntile, tile_body, 0)

        rows_per_iter = V7X_SUBLANES

        def gate_body(g, carry):
            base = pl.multiple_of(g * rows_per_iter, rows_per_iter)
            for h in range(heads):
                nt_scr[h] = n_scr[h, pl.ds(base, rows_per_iter), :]
                et_scr[h] = e1_scr[h, pl.ds(base, rows_per_iter), :]
            for lt in range(ntile):
                lanes = slice(lt * ln, (lt + 1) * ln)
                for il in range(0, rows_per_iter, 2):
                    g0 = jnp.zeros((nkeys, ln), BF16)
                    g1 = jnp.zeros((nkeys, ln), BF16)
                    for h in range(heads):
                        rank = rank_scr[h, :, lanes]
                        e2 = e2_scr[h, :, lanes]
                        n0 = nt_scr[h, il:il + 1, lanes].astype(BF16)
                        n1 = nt_scr[h, il + 1:il + 2, lanes].astype(BF16)
                        v0 = e2 * et_scr[h, il:il + 1, lanes].astype(BF16)
                        v1 = e2 * et_scr[h, il + 1:il + 2, lanes].astype(BF16)
                        g0 = g0 + jnp.where(rank < n0, v0, jnp.zeros_like(v0))
                        g1 = g1 + jnp.where(rank < n1, v1, jnp.zeros_like(v1))
                    r0 = pl.multiple_of((base + il) * nkeys, nkeys)
                    g_scr[pl.ds(r0, nkeys), lanes] = g0
                    g_scr[pl.ds(r0 + nkeys, nkeys), lanes] = g1
            return carry

        lax.fori_loop(0, nkeys // rows_per_iter, gate_body, 0)

    a_t = _dot_nt(u_ref[...], n2_ref[...])
    gl = (0.5 * a_t * (1.0 + lax.erf(a_t * INV_SQRT2))).astype(BF16)
    w_t = gl * g_scr[pl.ds(pl.multiple_of(j * ec, ec), ec), :]
    acc_scr[...] += jnp.dot(vt_ref[...], w_t, preferred_element_type=F32)

    @pl.when(j == pl.num_programs(1) - 1)
    def _finish():
        xo = h_ref[...] + jnp.transpose(acc_scr[...])
        y_ref[...] = _rmsnorm_rows(xo, fg_ref[...]) if final_norm else xo


def _peer(n2, pq, h, sk, u_bf, vt_bf, fg, *, heads, nkeys, tb, ec, final_norm):
    t, d = h.shape
    ne = u_bf.shape[0]
    nq = pq.shape[1]
    nchunk = ne // ec
    est = (2 * tb * d * 2 + 2 * tb * nq * 2 + 2 * tb * d * 4 + 2 * sk.size * 2 + 4 * ec * d * 2
           + 2 * tb * d * 4 + heads * nkeys * tb * (4 + 4 + 2 + 2) + 2 * heads * V7X_SUBLANES * tb * 4
           + ne * tb * 2 + d * tb * 4 + ec * tb * (4 + 2 + 2))
    return pl.pallas_call(
        functools.partial(_peer_kernel, heads=heads, nkeys=nkeys, tb=tb, ec=ec, final_norm=final_norm),
        grid=(t // tb, nchunk),
        in_specs=[
            pl.BlockSpec((tb, d), lambda i, j: (i, 0)),
            pl.BlockSpec((tb, nq), lambda i, j: (i, 0)),
            pl.BlockSpec((tb, d), lambda i, j: (i, 0)),
            pl.BlockSpec(sk.shape, lambda i, j: (0, 0, 0)),
            pl.BlockSpec((ec, d), lambda i, j: (j, 0)),
            pl.BlockSpec((d, ec), lambda i, j: (0, j)),
            pl.BlockSpec((1, d), lambda i, j: (0, 0)),
        ],
        out_specs=pl.BlockSpec((tb, d), lambda i, j: (i, 0)),
        out_shape=jax.ShapeDtypeStruct((t, d), F32),
        scratch_shapes=[
            pltpu.VMEM((heads, nkeys, tb), F32),
            pltpu.VMEM((heads, nkeys, tb), F32),
            pltpu.VMEM((heads, nkeys, tb), BF16),
            pltpu.VMEM((heads, nkeys, tb), BF16),
            pltpu.VMEM((heads, V7X_SUBLANES, tb), F32),
            pltpu.VMEM((heads, V7X_SUBLANES, tb), F32),
            pltpu.VMEM((ne, tb), BF16),
            pltpu.VMEM((d, tb), F32),
        ],
        compiler_params=pltpu.CompilerParams(
            dimension_semantics=("parallel", "arbitrary"), vmem_limit_bytes=_vmem_limit(est)),
        name="peer",
    )(n2, pq, h, sk, u_bf, vt_bf, fg.reshape(1, d))


def _layout(d, heads_a, kv_a, hd_a, heads_b, dk, dv, rank):
    widths = [("qa", heads_a * hd_a), ("vb", heads_b * dv), ("rb", heads_b * dv), ("ga", d), ("gb", d),
              ("qb", heads_b * dk), ("kb", heads_b * dk), ("kv", 2 * kv_a * hd_a), ("z", V7X_LANES)]
    lay, off = {}, 0
    for name, w in widths:
        assert off % w == 0
        lay[name] = (off, w)
        off += w
    assert rank <= V7X_LANES
    lay["ncol"] = off
    return lay


def _prep_w_in(w, cols, rank):
    qa, ka, va, qb, kb, vb, rb, zb, ga, gb = jnp.split(w, np.cumsum(cols)[:-1].tolist(), axis=1)
    zpad = jnp.pad(zb, ((0, 0), (0, V7X_LANES - rank)))
    return jnp.concatenate([qa, vb, rb, ga, gb, qb, kb, ka, va, zpad], axis=1).astype(BF16)


def kernel(x_prompt, x_sample, cache_k_win, cache_v_win, state_gla, norm1_g, w_in, w_gla_gate, b_gla_gate,
           attn_sinks, gla_norm_g, w_out, norm2_g, peer_wq, peer_subkeys, peer_u, peer_v, final_norm_g):
    batch, seq, d = x_prompt.shape
    nsamp, dec_seq, _ = x_sample.shape
    assert dec_seq == 1
    depth, _, window, kv_a, hd_a = cache_k_win.shape
    heads_a = attn_sinks.shape[1]
    group = heads_a // kv_a
    _, _, heads_b, dk, dv = state_gla.shape
    rank = w_gla_gate.shape[1]
    peer_heads, _, nkeys, dhalf = peer_subkeys.shape[1:]
    n_exp = peer_u.shape[1]
    assert n_exp == nkeys * nkeys and dhalf == nkeys == V7X_LANES
    cols = (heads_a * hd_a, kv_a * hd_a, kv_a * hd_a, heads_b * dk, heads_b * dk, heads_b * dv,
            heads_b * dv, rank, d, d)
    lay = _layout(d, heads_a, kv_a, hd_a, heads_b, dk, dv, rank)
    kvd = kv_a * hd_a

    yp = x_prompt.reshape(batch * seq, d)
    ys = x_sample.reshape(nsamp, d)
    outs = {k: [] for k in ("kp", "vp", "sp", "ks", "vs", "ss")}
    for l in range(depth):
        w_l = _prep_w_in(w_in[l], cols, rank)
        wg = jnp.pad(w_gla_gate[l], ((0, V7X_LANES - rank), (0, 0))).astype(BF16)
        wo = w_out[l].astype(BF16)
        wq = peer_wq[l].astype(BF16)
        sk = peer_subkeys[l].reshape(peer_heads * 2, nkeys, dhalf).astype(BF16)
        u_bf = peer_u[l].astype(BF16)
        vt_bf = jnp.transpose(peer_v[l]).astype(BF16)
        last = l == depth - 1

        proj_p, kv_p = _inproj(yp, norm1_g[l], w_l, lay, tm=512)
        oa_p = _swa_prompt(proj_p, kv_p, attn_sinks[l], lay, batch, seq, heads_a, kv_a, hd_a, window)
        og_p, s_p = _gla(proj_p, wg, b_gla_gate[l], gla_norm_g[l], lay, heads_b, dk, dv,
                         batch=batch, seq=seq, chunk=128)
        h_p, n2_p, pq_p = _post(yp, oa_p, og_p, proj_p, wo, norm2_g[l], wq, lay, tm=512)
        yp = _peer(n2_p, pq_p, h_p, sk, u_bf, vt_bf, final_norm_g, heads=peer_heads, nkeys=nkeys,
                   tb=512, ec=V7X_SUBLANES * nkeys, final_norm=last)
        kv_last = kv_p.reshape(batch, seq, 2 * kvd)[:, seq - window:, :]
        outs["kp"].append(kv_last[..., :kvd].reshape(batch, window, kv_a, hd_a))
        outs["vp"].append(kv_last[..., kvd:].reshape(batch, window, kv_a, hd_a))
        outs["sp"].append(s_p)

        proj_s, kv_s = _inproj(ys, norm1_g[l], w_l, lay, tm=nsamp)
        qa_s = proj_s[:, :heads_a * hd_a].reshape(nsamp, kv_a, group, hd_a)
        qrep = jnp.transpose(qa_s, (0, 2, 1, 3)).reshape(nsamp, group, 1, kvd)
        qrep = jnp.broadcast_to(qrep, (nsamp, group, kv_a, kvd)).reshape(nsamp, heads_a, kvd)
        row_head = np.array([(r % kv_a) * group + r // kv_a for r in range(heads_a)])
        sink_col = attn_sinks[l][row_head].reshape(heads_a, 1)
        slope_col = jnp.asarray(2.0 ** (-8.0 * (row_head + 1) / heads_a), F32).reshape(heads_a, 1)
        nk_s, nv_s, o_rows = _swa_sample(
            qrep, cache_k_win[l].reshape(nsamp, window, kvd), cache_v_win[l].reshape(nsamp, window, kvd),
            kv_s, sink_col, slope_col, kv_a, hd_a, window)
        oa_s = o_rows[:, ::kv_a, :].reshape(nsamp, group, kv_a, hd_a)
        oa_s = jnp.transpose(oa_s, (0, 2, 1, 3)).reshape(nsamp, d)
        og_s, s_s = _gla(proj_s, wg, b_gla_gate[l], gla_norm_g[l], lay, heads_b, dk, dv,
                         batch=nsamp, seq=1, chunk=V7X_SUBLANES, s0=state_gla[l])
        h_s, n2_s, pq_s = _post(ys, oa_s, og_s, proj_s, wo, norm2_g[l], wq, lay, tm=nsamp)
        ys = _peer(n2_s, pq_s, h_s, sk, u_bf, vt_bf, final_norm_g, heads=peer_heads, nkeys=nkeys,
                   tb=nsamp, ec=V7X_SUBLANES * nkeys, final_norm=last)
        outs["ks"].append(nk_s.reshape(nsamp, window, kv_a, hd_a))
        outs["vs"].append(nv_s.reshape(nsamp, window, kv_a, hd_a))
        outs["ss"].append(s_s)

    return (yp.reshape(batch, seq, d), ys.reshape(nsamp, 1, d),
            jnp.stack(outs["kp"]), jnp.stack(outs["vp"]), jnp.stack(outs["sp"]),
            jnp.stack(outs["ks"]), jnp.stack(outs["vs"]), jnp.stack(outs["ss"]))
```

```python
import functools
import math

import numpy as np
import jax
import jax.numpy as jnp
from jax import lax
from jax.experimental import pallas as pl
from jax.experimental.pallas import tpu as pltpu

F32 = jnp.float32
BF16 = jnp.bfloat16

EPS = 1e-6
GATE_TEMP = 16.0
PEER_TOPK = 16
LOG2E = 1.4426950408889634
INV_SQRT2 = 0.7071067811865476

V7X_LANES = 128
V7X_SUBLANES = 8
V7X_MXU_DIM = 256
V7X_VMEM_BYTES = 64 * 1024 * 1024
MIB = 1024 * 1024


def _vmem_limit(estimate_bytes):
    return int(min(estimate_bytes + 16 * MIB, V7X_VMEM_BYTES - 8 * MIB))


def _dot_nt(a, b):
    return lax.dot_general(a, b, (((1,), (1,)), ((), ())), preferred_element_type=F32)


def _sigmoid(z):
    return 1.0 / (1.0 + jnp.exp(-z))


def _rmsnorm_rows(x, g):
    ms = jnp.mean(x * x, axis=-1, keepdims=True)
    return x * lax.rsqrt(ms + EPS) * g


def _inproj_kernel(x_ref, g_ref, w_ref, proj_ref, kv_ref, *, chunks, kv_chunk):
    n = _rmsnorm_rows(x_ref[...], g_ref[...]).astype(BF16)
    for c0, c1 in chunks:
        acc = jnp.dot(n, w_ref[:, c0:c1], preferred_element_type=F32)
        proj_ref[:, c0:c1] = acc.astype(BF16)
        if (c0, c1) == kv_chunk:
            kv_ref[...] = acc


def _inproj(x2d, g, w, lay, tm):
    t, d = x2d.shape
    ncol = w.shape[1]
    kv0, kvw = lay["kv"]
    step = 512
    chunks = tuple((c, min(c + step, ncol)) for c in range(0, ncol, step))
    assert (kv0, kv0 + kvw) in chunks
    est = 2 * tm * d * 4 + 2 * d * ncol * 2 + 2 * tm * ncol * 2 + 2 * tm * kvw * 4
    return pl.pallas_call(
        functools.partial(_inproj_kernel, chunks=chunks, kv_chunk=(kv0, kv0 + kvw)),
        grid=(t // tm,),
        in_specs=[
            pl.BlockSpec((tm, d), lambda i: (i, 0)),
            pl.BlockSpec((1, d), lambda i: (0, 0)),
            pl.BlockSpec((d, ncol), lambda i: (0, 0)),
        ],
        out_specs=[
            pl.BlockSpec((tm, ncol), lambda i: (i, 0)),
            pl.BlockSpec((tm, kvw), lambda i: (i, 0)),
        ],
        out_shape=[
            jax.ShapeDtypeStruct((t, ncol), BF16),
            jax.ShapeDtypeStruct((t, kvw), F32),
        ],
        compiler_params=pltpu.CompilerParams(
            dimension_semantics=("parallel",), vmem_limit_bytes=_vmem_limit(est)),
        name="inproj",
    )(x2d, g.reshape(1, d), w)


def _swa_prompt_kernel(sink_ref, q_ref, kvc_ref, kvp_ref, o_ref, *, heads, kv_heads, hd, window, slopes):
    j = pl.program_id(1)
    group = heads // kv_heads
    kvd = kv_heads * hd
    kvc = kvc_ref[...]
    kvp = kvp_ref[...]
    k_all = jnp.concatenate([kvp[:, :kvd], kvc[:, :kvd]], axis=0).astype(BF16)
    v_all = jnp.concatenate([kvp[:, kvd:], kvc[:, kvd:]], axis=0).astype(BF16)
    t_idx = lax.broadcasted_iota(jnp.int32, (window, 2 * window), 0)
    s_idx = lax.broadcasted_iota(jnp.int32, (window, 2 * window), 1)
    dist = window + t_idx - s_idx
    valid = (dist >= 0) & (dist < window) & ((s_idx >= window) | (j > 0))
    distf = dist.astype(F32)
    scale = hd ** -0.5
    for h in range(heads):
        kvh = h // group
        qh = q_ref[:, h * hd:(h + 1) * hd]
        kh = k_all[:, kvh * hd:(kvh + 1) * hd]
        vh = v_all[:, kvh * hd:(kvh + 1) * hd]
        s = _dot_nt(qh, kh) * scale - slopes[h] * distf
        s = jnp.where(valid, s, -jnp.inf)
        sink = sink_ref[h]
        m = jnp.maximum(jnp.max(s, axis=-1, keepdims=True), sink)
        p = jnp.exp(s - m)
        denom = jnp.sum(p, axis=-1, keepdims=True) + jnp.exp(sink - m)
        oh = jnp.dot((p / denom).astype(BF16), vh, preferred_element_type=F32)
        o_ref[:, h * hd:(h + 1) * hd] = oh.astype(o_ref.dtype)


def _swa_prompt(proj, kv, sinks, lay, batch, seq, heads, kv_heads, hd, window):
    t = batch * seq
    nb = seq // window
    q0, qw = lay["qa"]
    assert q0 == 0
    kvw = kv.shape[1]
    slopes = tuple(float(2.0 ** (-8.0 * (h + 1) / heads)) for h in range(heads))
    est = 2 * window * qw * 2 * 2 + 4 * window * kvw * 4
    return pl.pallas_call(
        functools.partial(_swa_prompt_kernel, heads=heads, kv_heads=kv_heads, hd=hd,
                          window=window, slopes=slopes),
        grid=(batch, nb),
        in_specs=[
            pl.BlockSpec(memory_space=pltpu.SMEM),
            pl.BlockSpec((window, qw), lambda b, j: (b * nb + j, 0)),
            pl.BlockSpec((window, kvw), lambda b, j: (b * nb + j, 0)),
            pl.BlockSpec((window, kvw), lambda b, j: (b * nb + jnp.maximum(j - 1, 0), 0)),
        ],
        out_specs=pl.BlockSpec((window, qw), lambda b, j: (b * nb + j, 0)),
        out_shape=jax.ShapeDtypeStruct((t, qw), BF16),
        compiler_params=pltpu.CompilerParams(
            dimension_semantics=("parallel", "parallel"), vmem_limit_bytes=_vmem_limit(est)),
        name="swa_prompt",
    )(sinks, proj, kv, kv)


def _swa_sample_kernel(qrep_ref, ck_ref, cv_ref, kvn_ref, sink_ref, slope_ref, nk_ref, nv_ref, o_ref,
                       *, bs, kv_heads, hd, window):
    kvd = kv_heads * hd
    rows = qrep_ref.shape[1]
    row_idx = lax.broadcasted_iota(jnp.int32, (window, kvd), 0)
    lane_blk = lax.broadcasted_iota(jnp.int32, (rows, kvd), 1) // hd
    row_kv = lax.broadcasted_iota(jnp.int32, (rows, kvd), 0) % kv_heads
    head_mask = lane_blk == row_kv
    dist = (window - 1 - lax.broadcasted_iota(jnp.int32, (rows, window), 1)).astype(F32)
    sink = sink_ref[...]
    slope = slope_ref[...]
    scale = hd ** -0.5
    for i in range(bs):
        knew = kvn_ref[i:i + 1, :kvd]
        vnew = kvn_ref[i:i + 1, kvd:]
        nk = jnp.where(row_idx == window - 1, knew, pltpu.roll(ck_ref[i], window - 1, axis=0))
        nv = jnp.where(row_idx == window - 1, vnew, pltpu.roll(cv_ref[i], window - 1, axis=0))
        nk_ref[i] = nk
        nv_ref[i] = nv
        qe = jnp.where(head_mask, qrep_ref[i], jnp.zeros_like(qrep_ref[i]))
        s = _dot_nt(qe, nk.astype(BF16)) * scale - slope * dist
        m = jnp.maximum(jnp.max(s, axis=-1, keepdims=True), sink)
        p = jnp.exp(s - m)
        denom = jnp.sum(p, axis=-1, keepdims=True) + jnp.exp(sink - m)
        r = jnp.dot((p / denom).astype(BF16), nv.astype(BF16), preferred_element_type=F32)
        r = jnp.where(head_mask, r, 0.0)
        shift = 1
        while shift < kv_heads:
            r = r + pltpu.roll(r, rows - shift, axis=0)
            shift *= 2
        o_ref[i] = r


def _swa_sample(qrep, ck, cv, kvn, sink_col, slope_col, kv_heads, hd, window, bs=8):
    s, rows, kvd = qrep.shape
    est = 2 * bs * (rows * kvd * 2 + 4 * window * kvd * 4 + 2 * kvd * 4 + rows * kvd * 4)
    return pl.pallas_call(
        functools.partial(_swa_sample_kernel, bs=bs, kv_heads=kv_heads, hd=hd, window=window),
        grid=(s // bs,),
        in_specs=[
            pl.BlockSpec((bs, rows, kvd), lambda i: (i, 0, 0)),
            pl.BlockSpec((bs, window, kvd), lambda i: (i, 0, 0)),
            pl.BlockSpec((bs, window, kvd), lambda i: (i, 0, 0)),
            pl.BlockSpec((bs, 2 * kvd), lambda i: (i, 0)),
            pl.BlockSpec((rows, 1), lambda i: (0, 0)),
            pl.BlockSpec((rows, 1), lambda i: (0, 0)),
        ],
        out_specs=[
            pl.BlockSpec((bs, window, kvd), lambda i: (i, 0, 0)),
            pl.BlockSpec((bs, window, kvd), lambda i: (i, 0, 0)),
            pl.BlockSpec((bs, rows, kvd), lambda i: (i, 0, 0)),
        ],
        out_shape=[
            jax.ShapeDtypeStruct((s, window, kvd), F32),
            jax.ShapeDtypeStruct((s, window, kvd), F32),
            jax.ShapeDtypeStruct((s, rows, kvd), F32),
        ],
        compiler_params=pltpu.CompilerParams(
            dimension_semantics=("parallel",), vmem_limit_bytes=_vmem_limit(est)),
        name="swa_sample",
    )(qrep, ck, cv, kvn, sink_col, slope_col)


def _gla_tables(c):
    nlev = int(round(math.log2(c)))
    assert 1 << nlev == c
    t = np.arange(c)
    u = t[None, :]
    rows = [u <= t[:, None], u > t[:, None]]
    lvl = np.full((c, c), -1, np.int32)
    lvl[t, t] = 0
    for l in range(1, nlev + 1):
        hs = 1 << (l - 1)
        pos = t % (2 * hs)
        right = pos >= hs
        bnd = (t - pos + hs - 1)[:, None]
        rows.append(right[:, None] & (u > bnd) & (u <= t[:, None]))
        rows.append((~right)[:, None] & (u > t[:, None]) & (u <= bnd))
        same = (t[:, None] // (2 * hs)) == (t[None, :] // (2 * hs))
        lvl[same & right[:, None] & (~right)[None, :]] = l
    mall = np.concatenate(rows, axis=0).astype(np.float32)
    return jnp.asarray(mall, dtype=BF16), jnp.asarray(lvl), nlev


def _gla_kernel(*refs, c, nlev, heads, dk, dv, sample_mode):
    if sample_mode:
        (q_ref, k_ref, v_ref, r_ref, z_ref, wg_ref, bg_ref, gn_ref, mall_ref, lvl_ref, s0_ref,
         og_ref, sout_ref, s_scr) = refs
    else:
        (q_ref, k_ref, v_ref, r_ref, z_ref, wg_ref, bg_ref, gn_ref, mall_ref, lvl_ref,
         og_ref, sout_ref, s_scr) = refs
    step = pl.program_id(1)

    if sample_mode:
        r0 = pl.program_id(0) % c
        row_mask = lax.broadcasted_iota(jnp.int32, (c, 1), 0) == r0
        s_scr[...] = s0_ref[0]
    else:
        @pl.when(step == 0)
        def _():
            s_scr[...] = jnp.zeros_like(s_scr)

    z = jnp.dot(z_ref[...], wg_ref[...], preferred_element_type=F32) + bg_ref[...]
    la_all = -(jnp.maximum(-z, 0.0) + jnp.log(1.0 + jnp.exp(-jnp.abs(z)))) / GATE_TEMP
    if sample_mode:
        la_all = jnp.where(row_mask, la_all, 0.0)
    lvl = lvl_ref[...]
    mall = mall_ref[...]
    gn = gn_ref[...]
    outs = []
    for hh in range(heads):
        la = la_all[:, hh * dk:(hh + 1) * dk]
        la_hi = la.astype(BF16)
        la_lo = (la - la_hi.astype(F32)).astype(BF16)
        e2 = jnp.dot(mall, jnp.concatenate([la_hi, la_lo], axis=1), preferred_element_type=F32)
        e = e2[:, :dk] + e2[:, dk:]
        b = e[0:c]
        q = q_ref[:, hh * dk:(hh + 1) * dk].astype(F32) * (dk ** -0.5)
        k = k_ref[:, hh * dk:(hh + 1) * dk].astype(F32)
        v = v_ref[:, hh * dv:(hh + 1) * dv]
        if sample_mode:
            k = jnp.where(row_mask, k, 0.0)
            v = jnp.where(row_mask, v, jnp.zeros_like(v))
        scores = jnp.where(lvl == 0, _dot_nt(q.astype(BF16), k.astype(BF16)), 0.0)
        for l in range(1, nlev + 1):
            eq = jnp.exp(e[2 * l * c:(2 * l + 1) * c])
            ek = jnp.exp(e[(2 * l + 1) * c:(2 * l + 2) * c])
            sc = _dot_nt((q * eq).astype(BF16), (k * ek).astype(BF16))
            scores = scores + jnp.where(lvl == l, sc, 0.0)
        s_prev = s_scr[hh]
        o = jnp.dot(scores.astype(BF16), v, preferred_element_type=F32)
        o = o + jnp.dot((q * jnp.exp(b)).astype(BF16), s_prev.astype(BF16), preferred_element_type=F32)
        dec = jnp.transpose(jnp.broadcast_to(jnp.exp(b[c - 1:c, :]), (dk, dk)))
        dec = jnp.concatenate([dec] * (dv // dk), axis=1)
        khat_t = jnp.transpose(k * jnp.exp(e[c:2 * c])).astype(BF16)
        s_scr[hh] = dec * s_prev + jnp.dot(khat_t, v, preferred_element_type=F32)
        og = _rmsnorm_rows(o, gn)
        rg = r_ref[:, hh * dv:(hh + 1) * dv].astype(F32)
        outs.append(og * (rg * _sigmoid(rg)))
    og_all = jnp.concatenate(outs, axis=1)
    if sample_mode:
        @pl.when(r0 == 0)
        def _():
            og_ref[...] = jnp.zeros_like(og_ref)
        og_ref[...] = jnp.where(row_mask, og_all, og_ref[...])
        sout_ref[0] = s_scr[...]
    else:
        og_ref[...] = og_all.astype(og_ref.dtype)

        @pl.when(step == pl.num_programs(1) - 1)
        def _():
            sout_ref[0] = s_scr[...]


def _gla(proj, wg, bg, gn, lay, heads, dk, dv, *, batch, seq, chunk, s0=None):
    sample_mode = s0 is not None
    t = proj.shape[0]
    mall, lvl, nlev = _gla_tables(chunk)
    q0, qw = lay["qb"]
    k0, kw = lay["kb"]
    v0, vw = lay["vb"]
    r0_, rw = lay["rb"]
    z0, zw = lay["z"]
    if sample_mode:
        grid = (t, 1)
        row = lambda b, s: b // chunk
    else:
        nchunk = seq // chunk
        grid = (batch, nchunk)
        row = lambda b, s: b * nchunk + s
    in_specs = [
        pl.BlockSpec((chunk, qw), lambda b, s: (row(b, s), q0 // qw)),
        pl.BlockSpec((chunk, kw), lambda b, s: (row(b, s), k0 // kw)),
        pl.BlockSpec((chunk, vw), lambda b, s: (row(b, s), v0 // vw)),
        pl.BlockSpec((chunk, rw), lambda b, s: (row(b, s), r0_ // rw)),
        pl.BlockSpec((chunk, zw), lambda b, s: (row(b, s), z0 // zw)),
        pl.BlockSpec(wg.shape, lambda b, s: (0, 0)),
        pl.BlockSpec((1, heads * dk), lambda b, s: (0, 0)),
        pl.BlockSpec((1, dv), lambda b, s: (0, 0)),
        pl.BlockSpec(mall.shape, lambda b, s: (0, 0)),
        pl.BlockSpec(lvl.shape, lambda b, s: (0, 0)),
    ]
    args = [proj, proj, proj, proj, proj, wg, bg.reshape(1, -1), gn.reshape(1, -1), mall, lvl]
    nstate = t if sample_mode else batch
    if sample_mode:
        in_specs.append(pl.BlockSpec((1, heads, dk, dv), lambda b, s: (b, 0, 0, 0)))
        args.append(s0)
    og_dtype = F32 if sample_mode else BF16
    est = (2 * chunk * (qw + kw + vw + rw + zw) * 2 + 2 * mall.size * 2 + 2 * lvl.size * 4
           + 5 * heads * dk * dv * 4 + 2 * chunk * vw * 4 + 8 * mall.shape[0] * 2 * dk * 4)
    return pl.pallas_call(
        functools.partial(_gla_kernel, c=chunk, nlev=nlev, heads=heads, dk=dk, dv=dv,
                          sample_mode=sample_mode),
        grid=grid,
        in_specs=in_specs,
        out_specs=[
            pl.BlockSpec((chunk, heads * dv), lambda b, s: (row(b, s), 0)),
            pl.BlockSpec((1, heads, dk, dv), lambda b, s: (b, 0, 0, 0)),
        ],
        out_shape=[
            jax.ShapeDtypeStruct((t, heads * dv), og_dtype),
            jax.ShapeDtypeStruct((nstate, heads, dk, dv), F32),
        ],
        scratch_shapes=[pltpu.VMEM((heads, dk, dv), F32)],
        compiler_params=pltpu.CompilerParams(
            dimension_semantics=("arbitrary", "arbitrary"), vmem_limit_bytes=_vmem_limit(est)),
        name="gla_sample" if sample_mode else "gla_prompt",
    )(*args)


def _post_kernel(x_ref, oa_ref, og_ref, ga_ref, gb_ref, wo_ref, g2_ref, wq_ref, h_ref, n2_ref, pq_ref):
    u = (_sigmoid(ga_ref[...].astype(F32)) * oa_ref[...].astype(F32)
         + _sigmoid(gb_ref[...].astype(F32)) * og_ref[...].astype(F32))
    h = x_ref[...] + jnp.dot(u.astype(BF16), wo_ref[...], preferred_element_type=F32)
    h_ref[...] = h
    n2 = _rmsnorm_rows(h, g2_ref[...]).astype(BF16)
    n2_ref[...] = n2
    pq_ref[...] = jnp.dot(n2, wq_ref[...], preferred_element_type=F32).astype(BF16)


def _post(x2d, oa, og, proj, wo, g2, wq, lay, tm):
    t, d = x2d.shape
    ga0, gaw = lay["ga"]
    gb0, gbw = lay["gb"]
    nq = wq.shape[1]
    est = (2 * tm * d * (4 + oa.dtype.itemsize + og.dtype.itemsize + 2 + 2 + 4 + 2)
           + 2 * d * d * 2 + 2 * d * nq * 2 + 2 * tm * nq * 2 + tm * nq * 4)
    return pl.pallas_call(
        _post_kernel,
        grid=(t // tm,),
        in_specs=[
            pl.BlockSpec((tm, d), lambda i: (i, 0)),
            pl.BlockSpec((tm, d), lambda i: (i, 0)),
            pl.BlockSpec((tm, d), lambda i: (i, 0)),
            pl.BlockSpec((tm, gaw), lambda i: (i, ga0 // gaw)),
            pl.BlockSpec((tm, gbw), lambda i: (i, gb0 // gbw)),
            pl.BlockSpec((d, d), lambda i: (0, 0)),
            pl.BlockSpec((1, d), lambda i: (0, 0)),
            pl.BlockSpec((d, nq), lambda i: (0, 0)),
        ],
        out_specs=[
            pl.BlockSpec((tm, d), lambda i: (i, 0)),
            pl.BlockSpec((tm, d), lambda i: (i, 0)),
            pl.BlockSpec((tm, nq), lambda i: (i, 0)),
        ],
        out_shape=[
            jax.ShapeDtypeStruct((t, d), F32),
            jax.ShapeDtypeStruct((t, d), BF16),
            jax.ShapeDtypeStruct((t, nq), BF16),
        ],
        compiler_params=pltpu.CompilerParams(
            dimension_semantics=("parallel",), vmem_limit_bytes=_vmem_limit(est)),
        name="post",
    )(x2d, oa, og, proj, proj, wo, g2.reshape(1, d), wq)


def _oddeven_mergesort_pairs(n):
    pairs = []
    p = 1
    while p < n:
        k = p
        while k >= 1:
            for j in range(k % p, n - k, 2 * k):
                for i in range(min(k, n - j - k)):
                    if (i + j) // (2 * p) == (i + j + k) // (2 * p):
                        pairs.append((i + j, i + j + k))
            k //= 2
        p *= 2
    return pairs


_SORT16_PAIRS = tuple(_oddeven_mergesort_pairs(PEER_TOPK))


def _cmpx(xs, i, j):
    a, b = xs[i], xs[j]
    if b is None:
        return
    if a is None:
        xs[i], xs[j] = b, None
        return
    xs[i], xs[j] = jnp.maximum(a, b), jnp.minimum(a, b)


def _bitonic_sort_desc(xs):
    n = len(xs)
    k = n // 2
    while k >= 1:
        for i in range(n):
            if i & k == 0:
                _cmpx(xs, i, i + k)
        k //= 2
    return xs


def _top_set(x, y):
    n = PEER_TOPK
    x = list(x) + [None] * (n - len(x))
    y = list(y) + [None] * (n - len(y))
    out = []
    for i in range(n):
        a, b = x[i], y[n - 1 - i]
        out.append(b if a is None else (a if b is None else jnp.maximum(a, b)))
    return out


def _merge_desc(x, y):
    return [v for v in _bitonic_sort_desc(_top_set(x, y)) if v is not None]


def _top16_of_keys(s_t):
    nslab = s_t.shape[0] // V7X_SUBLANES
    assert nslab == PEER_TOPK
    xs = [s_t[i * V7X_SUBLANES:(i + 1) * V7X_SUBLANES, :] for i in range(nslab)]
    for i, j in _SORT16_PAIRS:
        _cmpx(xs, i, j)
    shift = V7X_SUBLANES // 2
    while shift >= 1:
        ys = [pltpu.roll(xs[PEER_TOPK - 1 - i], shift, axis=0) for i in range(PEER_TOPK)]
        xs = _bitonic_sort_desc([jnp.maximum(a, b) for a, b in zip(xs, ys)])
        shift //= 2
    return xs


def _pair_sum_top_set(a, b):
    n = PEER_TOPK
    row = lambda i, j0, j1: [a[i] + b[j] for j in range(j0, j1)]
    col = lambda j, i0, i1: [a[i] + b[j] for i in range(i0, i1)]
    top_rows = _merge_desc(_merge_desc(row(0, 0, n), row(1, 0, n // 2)),
                           _merge_desc(row(2, 0, n // 3), row(3, 0, n // 4)))
    rest = _merge_desc(_merge_desc(col(0, 4, n), col(1, 4, n // 2)), col(2, 4, n // 3))
    return _top_set(top_rows, rest)


def _peer_kernel(n2_ref, pq_ref, h_ref, sk_ref, u_ref, vt_ref, fg_ref, y_ref,
                 n_scr, e1_scr, rank_scr, e2_scr, nt_scr, et_scr, g_scr, acc_scr,
                 *, heads, nkeys, tb, ec, final_norm):
    j = pl.program_id(1)
    ln = V7X_LANES
    ntile = tb // ln
    sub = lax.broadcasted_iota(jnp.int32, (V7X_SUBLANES, ln), 0)
    assert heads == V7X_SUBLANES

    @pl.when(j == 0)
    def _route():
        acc_scr[...] = jnp.zeros_like(acc_scr)

        def tile_body(lt, carry):
            l0 = pl.multiple_of(lt * ln, ln)
            a_pack = [None] * PEER_TOPK
            b_pack = [None] * PEER_TOPK
            for h in range(heads):
                for c in range(2):
                    col = (h * 2 + c) * nkeys
                    qt = pq_ref[pl.ds(l0, ln), col:col + nkeys]
                    s_t = _dot_nt(sk_ref[h * 2 + c], qt) * LOG2E
                    if c == 0:
                        n_scr[h, :, pl.ds(l0, ln)] = s_t
                    else:
                        e1_scr[h, :, pl.ds(l0, ln)] = s_t
                    top = _top16_of_keys(s_t)
                    pack = a_pack if c == 0 else b_pack
                    for r in range(PEER_TOPK):
                        pack[r] = top[r] if h == 0 else jnp.where(sub == h, top[r], pack[r])
            top_s = _pair_sum_top_set(a_pack, b_pack)
            m = a_pack[0] + b_pack[0]
            zsum = None
            for v in top_s:
                e = jnp.exp2(v - m)
                zsum = e if zsum is None else zsum + e
            shift = m + jnp.log2(zsum)
            d_pack = [a - shift for a in a_pack]
            top_u = _pair_sum_top_set(d_pack, b_pack)
            thr = top_u[0]
            for v in top_u[1:]:
                thr = jnp.minimum(thr, v)
            for h in range(heads):
                b_rows = [b_pack[r][h:h + 1, :] for r in range(PEER_TOPK)]
                d1 = n_scr[h, :, pl.ds(l0, ln)] - shift[h:h + 1, :]
                s2 = e1_scr[h, :, pl.ds(l0, ln)]
                thr_h = thr[h:h + 1, :]
                n = jnp.zeros((nkeys, ln), F32)
                rank = jnp.zeros((nkeys, ln), F32)
                for r in range(PEER_TOPK):
                    n = jnp.where(b_rows[r] + d1 >= thr_h, float(r + 1), n)
                    rank = jnp.where(s2 < b_rows[r], float(r + 1), rank)
                n_scr[h, :, pl.ds(l0, ln)] = n
                e1_scr[h, :, pl.ds(l0, ln)] = jnp.exp2(d1 + b_rows[0])
                rank_scr[h, :, pl.ds(l0, ln)] = rank.astype(rank_scr.dtype)
                e2_scr[h, :, pl.ds(l0, ln)] = jnp.exp2(s2 - b_rows[0]).astype(e2_scr.dtype)
            return carry

        lax.fori_loop(0, ntile, tile_body, 0)

        rows_per_iter = V7X_SUBLANES

        def gate_body(g, carry):
            base = pl.multiple_of(g * rows_per_iter, rows_per_iter)
            for h in range(heads):
                nt_scr[h] = n_scr[h, pl.ds(base, rows_per_iter), :]
                et_scr[h] = e1_scr[h, pl.ds(base, rows_per_iter), :]
            for lt in range(ntile):
                lanes = slice(lt * ln, (lt + 1) * ln)
                for il in range(0, rows_per_iter, 2):
                    g0 = jnp.zeros((nkeys, ln), BF16)
                    g1 = jnp.zeros((nkeys, ln), BF16)
                    for h in range(heads):
                        rank = rank_scr[h, :, lanes]
                        e2 = e2_scr[h, :, lanes]
                        n0 = nt_scr[h, il:il + 1, lanes].astype(BF16)
                        n1 = nt_scr[h, il + 1:il + 2, lanes].astype(BF16)
                        v0 = e2 * et_scr[h, il:il + 1, lanes].astype(BF16)
                        v1 = e2 * et_scr[h, il + 1:il + 2, lanes].astype(BF16)
                        g0 = g0 + jnp.where(rank < n0, v0, jnp.zeros_like(v0))
                        g1 = g1 + jnp.where(rank < n1, v1, jnp.zeros_like(v1))
                    r0 = pl.multiple_of((base + il) * nkeys, nkeys)
                    g_scr[pl.ds(r0, nkeys), lanes] = g0
                    g_scr[pl.ds(r0 + nkeys, nkeys), lanes] = g1
            return carry

        lax.fori_loop(0, nkeys // rows_per_iter, gate_body, 0)

    a_t = _dot_nt(u_ref[...], n2_ref[...])
    gl = (0.5 * a_t * (1.0 + lax.erf(a_t * INV_SQRT2))).astype(BF16)
    w_t = gl * g_scr[pl.ds(pl.multiple_of(j * ec, ec), ec), :]
    acc_scr[...] += jnp.dot(vt_ref[0], w_t, preferred_element_type=F32)

    @pl.when(j == pl.num_programs(1) - 1)
    def _finish():
        xo = h_ref[...] + jnp.transpose(acc_scr[...])
        y_ref[...] = _rmsnorm_rows(xo, fg_ref[...]) if final_norm else xo


def _peer(n2, pq, h, sk, u_bf, vt_bf, fg, *, heads, nkeys, tb, final_norm):
    t, d = h.shape
    ec = vt_bf.shape[2]
    ne = u_bf.shape[0]
    nq = pq.shape[1]
    nchunk = ne // ec
    est = (2 * tb * d * 2 + 2 * tb * nq * 2 + 2 * tb * d * 4 + 2 * sk.size * 2 + 4 * ec * d * 2
           + 2 * tb * d * 4 + heads * nkeys * tb * (4 + 4 + 2 + 2) + 2 * heads * V7X_SUBLANES * tb * 4
           + ne * tb * 2 + d * tb * 4 + ec * tb * (4 + 2 + 2))
    return pl.pallas_call(
        functools.partial(_peer_kernel, heads=heads, nkeys=nkeys, tb=tb, ec=ec, final_norm=final_norm),
        grid=(t // tb, nchunk),
        in_specs=[
            pl.BlockSpec((tb, d), lambda i, j: (i, 0)),
            pl.BlockSpec((tb, nq), lambda i, j: (i, 0)),
            pl.BlockSpec((tb, d), lambda i, j: (i, 0)),
            pl.BlockSpec(sk.shape, lambda i, j: (0, 0, 0)),
            pl.BlockSpec((ec, d), lambda i, j: (j, 0)),
            pl.BlockSpec((1, d, ec), lambda i, j: (j, 0, 0)),
            pl.BlockSpec((1, d), lambda i, j: (0, 0)),
        ],
        out_specs=pl.BlockSpec((tb, d), lambda i, j: (i, 0)),
        out_shape=jax.ShapeDtypeStruct((t, d), F32),
        scratch_shapes=[
            pltpu.VMEM((heads, nkeys, tb), F32),
            pltpu.VMEM((heads, nkeys, tb), F32),
            pltpu.VMEM((heads, nkeys, tb), BF16),
            pltpu.VMEM((heads, nkeys, tb), BF16),
            pltpu.VMEM((heads, V7X_SUBLANES, tb), F32),
            pltpu.VMEM((heads, V7X_SUBLANES, tb), F32),
            pltpu.VMEM((ne, tb), BF16),
            pltpu.VMEM((d, tb), F32),
        ],
        compiler_params=pltpu.CompilerParams(
            dimension_semantics=("parallel", "arbitrary"), vmem_limit_bytes=_vmem_limit(est)),
        name="peer",
    )(n2, pq, h, sk, u_bf, vt_bf, fg.reshape(1, d))


def _layout(d, heads_a, kv_a, hd_a, heads_b, dk, dv, rank):
    widths = [("qa", heads_a * hd_a), ("vb", heads_b * dv), ("rb", heads_b * dv), ("ga", d), ("gb", d),
              ("qb", heads_b * dk), ("kb", heads_b * dk), ("kv", 2 * kv_a * hd_a), ("z", V7X_LANES)]
    lay, off = {}, 0
    for name, w in widths:
        assert off % w == 0
        lay[name] = (off, w)
        off += w
    assert rank <= V7X_LANES
    lay["ncol"] = off
    return lay


def _prep_w_in(w, cols, rank):
    qa, ka, va, qb, kb, vb, rb, zb, ga, gb = jnp.split(w, np.cumsum(cols)[:-1].tolist(), axis=1)
    zpad = jnp.pad(zb, ((0, 0), (0, V7X_LANES - rank)))
    return jnp.concatenate([qa, vb, rb, ga, gb, qb, kb, ka, va, zpad], axis=1).astype(BF16)


def kernel(x_prompt, x_sample, cache_k_win, cache_v_win, state_gla, norm1_g, w_in, w_gla_gate, b_gla_gate,
           attn_sinks, gla_norm_g, w_out, norm2_g, peer_wq, peer_subkeys, peer_u, peer_v, final_norm_g):
    batch, seq, d = x_prompt.shape
    nsamp, dec_seq, _ = x_sample.shape
    assert dec_seq == 1
    depth, _, window, kv_a, hd_a = cache_k_win.shape
    heads_a = attn_sinks.shape[1]
    group = heads_a // kv_a
    _, _, heads_b, dk, dv = state_gla.shape
    rank = w_gla_gate.shape[1]
    peer_heads, _, nkeys, dhalf = peer_subkeys.shape[1:]
    n_exp = peer_u.shape[1]
    assert n_exp == nkeys * nkeys and dhalf == nkeys == V7X_LANES
    cols = (heads_a * hd_a, kv_a * hd_a, kv_a * hd_a, heads_b * dk, heads_b * dk, heads_b * dv,
            heads_b * dv, rank, d, d)
    lay = _layout(d, heads_a, kv_a, hd_a, heads_b, dk, dv, rank)
    kvd = kv_a * hd_a

    yp = x_prompt.reshape(batch * seq, d)
    ys = x_sample.reshape(nsamp, d)
    outs = {k: [] for k in ("kp", "vp", "sp", "ks", "vs", "ss")}
    for l in range(depth):
        w_l = _prep_w_in(w_in[l], cols, rank)
        wg = jnp.pad(w_gla_gate[l], ((0, V7X_LANES - rank), (0, 0))).astype(BF16)
        wo = w_out[l].astype(BF16)
        wq = peer_wq[l].astype(BF16)
        sk = peer_subkeys[l].reshape(peer_heads * 2, nkeys, dhalf).astype(BF16)
        u_bf = peer_u[l].astype(BF16)
        ec = V7X_SUBLANES * nkeys
        vt_bf = jnp.transpose(peer_v[l].reshape(n_exp // ec, ec, d), (0, 2, 1)).astype(BF16)
        last = l == depth - 1

        proj_p, kv_p = _inproj(yp, norm1_g[l], w_l, lay, tm=512)
        oa_p = _swa_prompt(proj_p, kv_p, attn_sinks[l], lay, batch, seq, heads_a, kv_a, hd_a, window)
        og_p, s_p = _gla(proj_p, wg, b_gla_gate[l], gla_norm_g[l], lay, heads_b, dk, dv,
                         batch=batch, seq=seq, chunk=128)
        h_p, n2_p, pq_p = _post(yp, oa_p, og_p, proj_p, wo, norm2_g[l], wq, lay, tm=512)
        yp = _peer(n2_p, pq_p, h_p, sk, u_bf, vt_bf, final_norm_g, heads=peer_heads, nkeys=nkeys,
                   tb=512, final_norm=last)
        kv_last = kv_p.reshape(batch, seq, 2 * kvd)[:, seq - window:, :]
        outs["kp"].append(kv_last[..., :kvd].reshape(batch, window, kv_a, hd_a))
        outs["vp"].append(kv_last[..., kvd:].reshape(batch, window, kv_a, hd_a))
        outs["sp"].append(s_p)

        proj_s, kv_s = _inproj(ys, norm1_g[l], w_l, lay, tm=nsamp)
        qa_s = proj_s[:, :heads_a * hd_a].reshape(nsamp, kv_a, group, hd_a)
        qrep = jnp.transpose(qa_s, (0, 2, 1, 3)).reshape(nsamp, group, 1, kvd)
        qrep = jnp.broadcast_to(qrep, (nsamp, group, kv_a, kvd)).reshape(nsamp, heads_a, kvd)
        row_head = np.array([(r % kv_a) * group + r // kv_a for r in range(heads_a)])
        sink_col = attn_sinks[l][row_head].reshape(heads_a, 1)
        slope_col = jnp.asarray(2.0 ** (-8.0 * (row_head + 1) / heads_a), F32).reshape(heads_a, 1)
        nk_s, nv_s, o_rows = _swa_sample(
            qrep, cache_k_win[l].reshape(nsamp, window, kvd), cache_v_win[l].reshape(nsamp, window, kvd),
            kv_s, sink_col, slope_col, kv_a, hd_a, window)
        oa_s = o_rows[:, ::kv_a, :].reshape(nsamp, group, kv_a, hd_a)
        oa_s = jnp.transpose(oa_s, (0, 2, 1, 3)).reshape(nsamp, d)
        og_s, s_s = _gla(proj_s, wg, b_gla_gate[l], gla_norm_g[l], lay, heads_b, dk, dv,
                         batch=nsamp, seq=1, chunk=V7X_SUBLANES, s0=state_gla[l])
        h_s, n2_s, pq_s = _post(ys, oa_s, og_s, proj_s, wo, norm2_g[l], wq, lay, tm=nsamp)
        ys = _peer(n2_s, pq_s, h_s, sk, u_bf, vt_bf, final_norm_g, heads=peer_heads, nkeys=nkeys,
                   tb=nsamp, final_norm=last)
        outs["ks"].append(nk_s.reshape(nsamp, window, kv_a, hd_a))
        outs["vs"].append(nv_s.reshape(nsamp, window, kv_a, hd_a))
        outs["ss"].append(s_s)

    return (yp.reshape(batch, seq, d), ys.reshape(nsamp, 1, d),
            jnp.stack(outs["kp"]), jnp.stack(outs["vp"]), jnp.stack(outs["sp"]),
            jnp.stack(outs["ks"]), jnp.stack(outs["vs"]), jnp.stack(outs["ss"]))
```

```python
import functools
import math

import numpy as np
import jax
import jax.numpy as jnp
from jax import lax
from jax.experimental import pallas as pl
from jax.experimental.pallas import tpu as pltpu

F32 = jnp.float32
BF16 = jnp.bfloat16

EPS = 1e-6
GATE_TEMP = 16.0
PEER_TOPK = 16
LOG2E = 1.4426950408889634
INV_SQRT2 = 0.7071067811865476

V7X_LANES = 128
V7X_SUBLANES = 8
V7X_MXU_DIM = 256
V7X_VMEM_BYTES = 64 * 1024 * 1024
MIB = 1024 * 1024


def _vmem_limit(estimate_bytes):
    return int(min(estimate_bytes + 16 * MIB, V7X_VMEM_BYTES - 8 * MIB))


def _dot_nt(a, b):
    return lax.dot_general(a, b, (((1,), (1,)), ((), ())), preferred_element_type=F32)


def _sigmoid(z):
    return 1.0 / (1.0 + jnp.exp(-z))


def _rmsnorm_rows(x, g):
    ms = jnp.mean(x * x, axis=-1, keepdims=True)
    return x * lax.rsqrt(ms + EPS) * g


def _inproj_kernel(x_ref, g_ref, w_ref, proj_ref, kv_ref, *, chunks, kv_chunk):
    n = _rmsnorm_rows(x_ref[...], g_ref[...]).astype(BF16)
    for c0, c1 in chunks:
        acc = jnp.dot(n, w_ref[:, c0:c1], preferred_element_type=F32)
        proj_ref[:, c0:c1] = acc.astype(BF16)
        if (c0, c1) == kv_chunk:
            kv_ref[...] = acc


def _inproj(x2d, g, w, lay, tm):
    t, d = x2d.shape
    ncol = w.shape[1]
    kv0, kvw = lay["kv"]
    step = 512
    chunks = tuple((c, min(c + step, ncol)) for c in range(0, ncol, step))
    assert (kv0, kv0 + kvw) in chunks
    est = 2 * tm * d * 4 + 2 * d * ncol * 2 + 2 * tm * ncol * 2 + 2 * tm * kvw * 4
    return pl.pallas_call(
        functools.partial(_inproj_kernel, chunks=chunks, kv_chunk=(kv0, kv0 + kvw)),
        grid=(t // tm,),
        in_specs=[
            pl.BlockSpec((tm, d), lambda i: (i, 0)),
            pl.BlockSpec((1, d), lambda i: (0, 0)),
            pl.BlockSpec((d, ncol), lambda i: (0, 0)),
        ],
        out_specs=[
            pl.BlockSpec((tm, ncol), lambda i: (i, 0)),
            pl.BlockSpec((tm, kvw), lambda i: (i, 0)),
        ],
        out_shape=[
            jax.ShapeDtypeStruct((t, ncol), BF16),
            jax.ShapeDtypeStruct((t, kvw), F32),
        ],
        compiler_params=pltpu.CompilerParams(
            dimension_semantics=("parallel",), vmem_limit_bytes=_vmem_limit(est)),
        name="inproj",
    )(x2d, g.reshape(1, d), w)


def _swa_prompt_kernel(sink_ref, q_ref, kvc_ref, kvp_ref, o_ref, *, heads, kv_heads, hd, window, slopes):
    j = pl.program_id(1)
    group = heads // kv_heads
    kvd = kv_heads * hd
    kvc = kvc_ref[...]
    kvp = kvp_ref[...]
    k_all = jnp.concatenate([kvp[:, :kvd], kvc[:, :kvd]], axis=0).astype(BF16)
    v_all = jnp.concatenate([kvp[:, kvd:], kvc[:, kvd:]], axis=0).astype(BF16)
    t_idx = lax.broadcasted_iota(jnp.int32, (window, 2 * window), 0)
    s_idx = lax.broadcasted_iota(jnp.int32, (window, 2 * window), 1)
    dist = window + t_idx - s_idx
    valid = (dist >= 0) & (dist < window) & ((s_idx >= window) | (j > 0))
    distf = dist.astype(F32)
    scale = hd ** -0.5
    for h in range(heads):
        kvh = h // group
        qh = q_ref[:, h * hd:(h + 1) * hd]
        kh = k_all[:, kvh * hd:(kvh + 1) * hd]
        vh = v_all[:, kvh * hd:(kvh + 1) * hd]
        s = _dot_nt(qh, kh) * scale - slopes[h] * distf
        s = jnp.where(valid, s, -jnp.inf)
        sink = sink_ref[h]
        m = jnp.maximum(jnp.max(s, axis=-1, keepdims=True), sink)
        p = jnp.exp(s - m)
        denom = jnp.sum(p, axis=-1, keepdims=True) + jnp.exp(sink - m)
        oh = jnp.dot((p / denom).astype(BF16), vh, preferred_element_type=F32)
        o_ref[:, h * hd:(h + 1) * hd] = oh.astype(o_ref.dtype)


def _swa_prompt(proj, kv, sinks, lay, batch, seq, heads, kv_heads, hd, window):
    t = batch * seq
    nb = seq // window
    q0, qw = lay["qa"]
    assert q0 == 0
    kvw = kv.shape[1]
    slopes = tuple(float(2.0 ** (-8.0 * (h + 1) / heads)) for h in range(heads))
    est = 2 * window * qw * 2 * 2 + 4 * window * kvw * 4
    return pl.pallas_call(
        functools.partial(_swa_prompt_kernel, heads=heads, kv_heads=kv_heads, hd=hd,
                          window=window, slopes=slopes),
        grid=(batch, nb),
        in_specs=[
            pl.BlockSpec(memory_space=pltpu.SMEM),
            pl.BlockSpec((window, qw), lambda b, j: (b * nb + j, 0)),
            pl.BlockSpec((window, kvw), lambda b, j: (b * nb + j, 0)),
            pl.BlockSpec((window, kvw), lambda b, j: (b * nb + jnp.maximum(j - 1, 0), 0)),
        ],
        out_specs=pl.BlockSpec((window, qw), lambda b, j: (b * nb + j, 0)),
        out_shape=jax.ShapeDtypeStruct((t, qw), BF16),
        compiler_params=pltpu.CompilerParams(
            dimension_semantics=("parallel", "parallel"), vmem_limit_bytes=_vmem_limit(est)),
        name="swa_prompt",
    )(sinks, proj, kv, kv)


def _swa_sample_kernel(qrep_ref, ck_ref, cv_ref, kvn_ref, sink_ref, slope_ref, nk_ref, nv_ref, o_ref,
                       *, bs, kv_heads, hd, window):
    kvd = kv_heads * hd
    rows = qrep_ref.shape[1]
    row_idx = lax.broadcasted_iota(jnp.int32, (window, kvd), 0)
    lane_blk = lax.broadcasted_iota(jnp.int32, (rows, kvd), 1) // hd
    row_kv = lax.broadcasted_iota(jnp.int32, (rows, kvd), 0) % kv_heads
    head_mask = lane_blk == row_kv
    dist = (window - 1 - lax.broadcasted_iota(jnp.int32, (rows, window), 1)).astype(F32)
    sink = sink_ref[...]
    slope = slope_ref[...]
    scale = hd ** -0.5
    for i in range(bs):
        knew = kvn_ref[i:i + 1, :kvd]
        vnew = kvn_ref[i:i + 1, kvd:]
        nk = jnp.where(row_idx == window - 1, knew, pltpu.roll(ck_ref[i], window - 1, axis=0))
        nv = jnp.where(row_idx == window - 1, vnew, pltpu.roll(cv_ref[i], window - 1, axis=0))
        nk_ref[i] = nk
        nv_ref[i] = nv
        qe = jnp.where(head_mask, qrep_ref[i], jnp.zeros_like(qrep_ref[i]))
        s = _dot_nt(qe, nk.astype(BF16)) * scale - slope * dist
        m = jnp.maximum(jnp.max(s, axis=-1, keepdims=True), sink)
        p = jnp.exp(s - m)
        denom = jnp.sum(p, axis=-1, keepdims=True) + jnp.exp(sink - m)
        r = jnp.dot((p / denom).astype(BF16), nv.astype(BF16), preferred_element_type=F32)
        r = jnp.where(head_mask, r, 0.0)
        shift = 1
        while shift < kv_heads:
            r = r + pltpu.roll(r, rows - shift, axis=0)
            shift *= 2
        o_ref[i] = r


def _swa_sample(qrep, ck, cv, kvn, sink_col, slope_col, kv_heads, hd, window, bs=8):
    s, rows, kvd = qrep.shape
    est = 2 * bs * (rows * kvd * 2 + 4 * window * kvd * 4 + 2 * kvd * 4 + rows * kvd * 4)
    return pl.pallas_call(
        functools.partial(_swa_sample_kernel, bs=bs, kv_heads=kv_heads, hd=hd, window=window),
        grid=(s // bs,),
        in_specs=[
            pl.BlockSpec((bs, rows, kvd), lambda i: (i, 0, 0)),
            pl.BlockSpec((bs, window, kvd), lambda i: (i, 0, 0)),
            pl.BlockSpec((bs, window, kvd), lambda i: (i, 0, 0)),
            pl.BlockSpec((bs, 2 * kvd), lambda i: (i, 0)),
            pl.BlockSpec((rows, 1), lambda i: (0, 0)),
            pl.BlockSpec((rows, 1), lambda i: (0, 0)),
        ],
        out_specs=[
            pl.BlockSpec((bs, window, kvd), lambda i: (i, 0, 0)),
            pl.BlockSpec((bs, window, kvd), lambda i: (i, 0, 0)),
            pl.BlockSpec((bs, rows, kvd), lambda i: (i, 0, 0)),
        ],
        out_shape=[
            jax.ShapeDtypeStruct((s, window, kvd), F32),
            jax.ShapeDtypeStruct((s, window, kvd), F32),
            jax.ShapeDtypeStruct((s, rows, kvd), F32),
        ],
        compiler_params=pltpu.CompilerParams(
            dimension_semantics=("parallel",), vmem_limit_bytes=_vmem_limit(est)),
        name="swa_sample",
    )(qrep, ck, cv, kvn, sink_col, slope_col)


def _gla_tables(c):
    nlev = int(round(math.log2(c)))
    assert 1 << nlev == c
    t = np.arange(c)
    u = t[None, :]
    rows = [u <= t[:, None], u > t[:, None]]
    lvl = np.full((c, c), -1, np.int32)
    lvl[t, t] = 0
    for l in range(1, nlev + 1):
        hs = 1 << (l - 1)
        pos = t % (2 * hs)
        right = pos >= hs
        bnd = (t - pos + hs - 1)[:, None]
        rows.append(right[:, None] & (u > bnd) & (u <= t[:, None]))
        rows.append((~right)[:, None] & (u > t[:, None]) & (u <= bnd))
        same = (t[:, None] // (2 * hs)) == (t[None, :] // (2 * hs))
        lvl[same & right[:, None] & (~right)[None, :]] = l
    mall = np.concatenate(rows, axis=0).astype(np.float32)
    return jnp.asarray(mall, dtype=BF16), jnp.asarray(lvl), nlev


def _gla_kernel(*refs, c, nlev, heads, dk, dv, sample_mode):
    if sample_mode:
        (q_ref, k_ref, v_ref, r_ref, z_ref, wg_ref, bg_ref, gn_ref, mall_ref, lvl_ref, s0_ref,
         og_ref, sout_ref) = refs
    else:
        (q_ref, k_ref, v_ref, r_ref, z_ref, wg_ref, bg_ref, gn_ref, mall_ref, lvl_ref,
         og_ref, sout_ref, s_scr) = refs
        step = pl.program_id(1)

        @pl.when(step == 0)
        def _():
            s_scr[...] = jnp.zeros_like(s_scr)

    z = jnp.dot(z_ref[...], wg_ref[...], preferred_element_type=F32) + bg_ref[...]
    la_full = -(jnp.maximum(-z, 0.0) + jnp.log(1.0 + jnp.exp(-jnp.abs(z)))) / GATE_TEMP
    lvl = lvl_ref[...]
    mall = mall_ref[...]
    gn = gn_ref[...]
    row_idx = lax.broadcasted_iota(jnp.int32, (c, 1), 0)
    items = [(s, hh) for s in (range(c) if sample_mode else (None,)) for hh in range(heads)]
    mask_of = {s: (None if s is None else row_idx == s) for s, _ in items}

    def la_of(s, hh):
        la = la_full[:, hh * dk:(hh + 1) * dk]
        return la if s is None else jnp.where(mask_of[s], la, 0.0)

    def split(la):
        hi = la.astype(BF16)
        return jnp.concatenate([hi, (la - hi.astype(F32)).astype(BF16)], axis=1)

    e2s = [jnp.dot(mall, split(la_of(s, hh)), preferred_element_type=F32) for s, hh in items]
    es = [e2[:, :dk] + e2[:, dk:] for e2 in e2s]
    qs = [q_ref[:, hh * dk:(hh + 1) * dk].astype(F32) * (dk ** -0.5) for _, hh in items]
    ks, vs = [], []
    for s, hh in items:
        k = k_ref[:, hh * dk:(hh + 1) * dk].astype(F32)
        v = v_ref[:, hh * dv:(hh + 1) * dv]
        if s is not None:
            k = jnp.where(mask_of[s], k, 0.0)
            v = jnp.where(mask_of[s], v, jnp.zeros_like(v))
        ks.append(k)
        vs.append(v)
    scores = [jnp.where(lvl == 0, _dot_nt(q.astype(BF16), k.astype(BF16)), 0.0) for q, k in zip(qs, ks)]
    for l in range(1, nlev + 1):
        scs = [_dot_nt((q * jnp.exp(e[2 * l * c:(2 * l + 1) * c])).astype(BF16),
                       (k * jnp.exp(e[(2 * l + 1) * c:(2 * l + 2) * c])).astype(BF16))
               for q, k, e in zip(qs, ks, es)]
        scores = [acc + jnp.where(lvl == l, sc, 0.0) for acc, sc in zip(scores, scs)]
    s_prevs = [s_scr[hh] if s is None else s0_ref[s, hh] for s, hh in items]
    os_ = [jnp.dot(sc.astype(BF16), v, preferred_element_type=F32)
           + jnp.dot((q * jnp.exp(e[0:c])).astype(BF16), sp.astype(BF16), preferred_element_type=F32)
           for sc, v, q, e, sp in zip(scores, vs, qs, es, s_prevs)]
    decs = [jnp.concatenate([jnp.transpose(jnp.broadcast_to(jnp.exp(e[c - 1:c, :]), (dk, dk)))] * (dv // dk),
                            axis=1) for e in es]
    khats = [jnp.transpose(k * jnp.exp(e[c:2 * c])).astype(BF16) for k, e in zip(ks, es)]
    for (s, hh), dec, sp, kh, v in zip(items, decs, s_prevs, khats, vs):
        s_new = dec * sp + jnp.dot(kh, v, preferred_element_type=F32)
        if s is None:
            s_scr[hh] = s_new
        else:
            sout_ref[s, hh] = s_new
    for (s, hh), o in zip(items, os_):
        rg = r_ref[:, hh * dv:(hh + 1) * dv].astype(F32)
        og = _rmsnorm_rows(o, gn) * (rg * _sigmoid(rg))
        if s is None:
            og_ref[:, hh * dv:(hh + 1) * dv] = og.astype(og_ref.dtype)
        else:
            og_ref[s:s + 1, hh * dv:(hh + 1) * dv] = og[s:s + 1, :].astype(og_ref.dtype)
    if not sample_mode:
        @pl.when(step == pl.num_programs(1) - 1)
        def _():
            sout_ref[0] = s_scr[...]


def _gla(proj, wg, bg, gn, lay, heads, dk, dv, *, batch, seq, chunk, s0=None):
    sample_mode = s0 is not None
    t = proj.shape[0]
    mall, lvl, nlev = _gla_tables(chunk)
    q0, qw = lay["qb"]
    k0, kw = lay["kb"]
    v0, vw = lay["vb"]
    r0_, rw = lay["rb"]
    z0, zw = lay["z"]
    if sample_mode:
        grid = (t // chunk, 1)
        row = lambda b, s: b
        state_block = (chunk, heads, dk, dv)
    else:
        nchunk = seq // chunk
        grid = (batch, nchunk)
        row = lambda b, s: b * nchunk + s
        state_block = (1, heads, dk, dv)
    in_specs = [
        pl.BlockSpec((chunk, qw), lambda b, s: (row(b, s), q0 // qw)),
        pl.BlockSpec((chunk, kw), lambda b, s: (row(b, s), k0 // kw)),
        pl.BlockSpec((chunk, vw), lambda b, s: (row(b, s), v0 // vw)),
        pl.BlockSpec((chunk, rw), lambda b, s: (row(b, s), r0_ // rw)),
        pl.BlockSpec((chunk, zw), lambda b, s: (row(b, s), z0 // zw)),
        pl.BlockSpec(wg.shape, lambda b, s: (0, 0)),
        pl.BlockSpec((1, heads * dk), lambda b, s: (0, 0)),
        pl.BlockSpec((1, dv), lambda b, s: (0, 0)),
        pl.BlockSpec(mall.shape, lambda b, s: (0, 0)),
        pl.BlockSpec(lvl.shape, lambda b, s: (0, 0)),
    ]
    args = [proj, proj, proj, proj, proj, wg, bg.reshape(1, -1), gn.reshape(1, -1), mall, lvl]
    nstate = t if sample_mode else batch
    if sample_mode:
        in_specs.append(pl.BlockSpec(state_block, lambda b, s: (b, 0, 0, 0)))
        args.append(s0)
    og_dtype = F32 if sample_mode else BF16
    est = (2 * chunk * (qw + kw + vw + rw + zw) * 2 + 2 * mall.size * 2 + 2 * lvl.size * 4
           + (4 * chunk if sample_mode else 5) * heads * dk * dv * 4 + 2 * chunk * vw * 4
           + 8 * mall.shape[0] * 2 * dk * 4)
    return pl.pallas_call(
        functools.partial(_gla_kernel, c=chunk, nlev=nlev, heads=heads, dk=dk, dv=dv,
                          sample_mode=sample_mode),
        grid=grid,
        in_specs=in_specs,
        out_specs=[
            pl.BlockSpec((chunk, heads * dv), lambda b, s: (row(b, s), 0)),
            pl.BlockSpec(state_block, lambda b, s: (b, 0, 0, 0)),
        ],
        out_shape=[
            jax.ShapeDtypeStruct((t, heads * dv), og_dtype),
            jax.ShapeDtypeStruct((nstate, heads, dk, dv), F32),
        ],
        scratch_shapes=[] if sample_mode else [pltpu.VMEM((heads, dk, dv), F32)],
        compiler_params=pltpu.CompilerParams(
            dimension_semantics=("arbitrary", "arbitrary"), vmem_limit_bytes=_vmem_limit(est)),
        name="gla_sample" if sample_mode else "gla_prompt",
    )(*args)


def _post_kernel(x_ref, oa_ref, og_ref, ga_ref, gb_ref, wo_ref, g2_ref, wq_ref, h_ref, n2_ref, pq_ref):
    u = (_sigmoid(ga_ref[...].astype(F32)) * oa_ref[...].astype(F32)
         + _sigmoid(gb_ref[...].astype(F32)) * og_ref[...].astype(F32))
    h = x_ref[...] + jnp.dot(u.astype(BF16), wo_ref[...], preferred_element_type=F32)
    h_ref[...] = h
    n2 = _rmsnorm_rows(h, g2_ref[...]).astype(BF16)
    n2_ref[...] = n2
    pq_ref[...] = jnp.dot(n2, wq_ref[...], preferred_element_type=F32).astype(BF16)


def _post(x2d, oa, og, proj, wo, g2, wq, lay, tm):
    t, d = x2d.shape
    ga0, gaw = lay["ga"]
    gb0, gbw = lay["gb"]
    nq = wq.shape[1]
    est = (2 * tm * d * (4 + oa.dtype.itemsize + og.dtype.itemsize + 2 + 2 + 4 + 2)
           + 2 * d * d * 2 + 2 * d * nq * 2 + 2 * tm * nq * 2 + tm * nq * 4)
    return pl.pallas_call(
        _post_kernel,
        grid=(t // tm,),
        in_specs=[
            pl.BlockSpec((tm, d), lambda i: (i, 0)),
            pl.BlockSpec((tm, d), lambda i: (i, 0)),
            pl.BlockSpec((tm, d), lambda i: (i, 0)),
            pl.BlockSpec((tm, gaw), lambda i: (i, ga0 // gaw)),
            pl.BlockSpec((tm, gbw), lambda i: (i, gb0 // gbw)),
            pl.BlockSpec((d, d), lambda i: (0, 0)),
            pl.BlockSpec((1, d), lambda i: (0, 0)),
            pl.BlockSpec((d, nq), lambda i: (0, 0)),
        ],
        out_specs=[
            pl.BlockSpec((tm, d), lambda i: (i, 0)),
            pl.BlockSpec((tm, d), lambda i: (i, 0)),
            pl.BlockSpec((tm, nq), lambda i: (i, 0)),
        ],
        out_shape=[
            jax.ShapeDtypeStruct((t, d), F32),
            jax.ShapeDtypeStruct((t, d), BF16),
            jax.ShapeDtypeStruct((t, nq), BF16),
        ],
        compiler_params=pltpu.CompilerParams(
            dimension_semantics=("parallel",), vmem_limit_bytes=_vmem_limit(est)),
        name="post",
    )(x2d, oa, og, proj, proj, wo, g2.reshape(1, d), wq)


def _oddeven_mergesort_pairs(n):
    pairs = []
    p = 1
    while p < n:
        k = p
        while k >= 1:
            for j in range(k % p, n - k, 2 * k):
                for i in range(min(k, n - j - k)):
                    if (i + j) // (2 * p) == (i + j + k) // (2 * p):
                        pairs.append((i + j, i + j + k))
            k //= 2
        p *= 2
    return pairs


_SORT16_PAIRS = tuple(_oddeven_mergesort_pairs(PEER_TOPK))


def _cmpx(xs, i, j):
    a, b = xs[i], xs[j]
    if b is None:
        return
    if a is None:
        xs[i], xs[j] = b, None
        return
    xs[i], xs[j] = jnp.maximum(a, b), jnp.minimum(a, b)


def _bitonic_sort_desc(xs):
    n = len(xs)
    k = n // 2
    while k >= 1:
        for i in range(n):
            if i & k == 0:
                _cmpx(xs, i, i + k)
        k //= 2
    return xs


def _top_set(x, y):
    n = PEER_TOPK
    x = list(x) + [None] * (n - len(x))
    y = list(y) + [None] * (n - len(y))
    out = []
    for i in range(n):
        a, b = x[i], y[n - 1 - i]
        out.append(b if a is None else (a if b is None else jnp.maximum(a, b)))
    return out


def _merge_desc(x, y):
    return [v for v in _bitonic_sort_desc(_top_set(x, y)) if v is not None]


def _top16_of_keys(s_t):
    nslab = s_t.shape[0] // V7X_SUBLANES
    assert nslab == PEER_TOPK
    xs = [s_t[i * V7X_SUBLANES:(i + 1) * V7X_SUBLANES, :] for i in range(nslab)]
    for i, j in _SORT16_PAIRS:
        _cmpx(xs, i, j)
    shift = V7X_SUBLANES // 2
    while shift >= 1:
        ys = [pltpu.roll(xs[PEER_TOPK - 1 - i], shift, axis=0) for i in range(PEER_TOPK)]
        xs = _bitonic_sort_desc([jnp.maximum(a, b) for a, b in zip(xs, ys)])
        shift //= 2
    return xs


def _pair_sum_top_set(a, b):
    n = PEER_TOPK
    row = lambda i, j0, j1: [a[i] + b[j] for j in range(j0, j1)]
    col = lambda j, i0, i1: [a[i] + b[j] for i in range(i0, i1)]
    top_rows = _merge_desc(_merge_desc(row(0, 0, n), row(1, 0, n // 2)),
                           _merge_desc(row(2, 0, n // 3), row(3, 0, n // 4)))
    rest = _merge_desc(_merge_desc(col(0, 4, n), col(1, 4, n // 2)), col(2, 4, n // 3))
    return _top_set(top_rows, rest)


def _peer_kernel(n2_ref, pq_ref, h_ref, sk_ref, u_ref, vt_ref, fg_ref, y_ref,
                 s2_scr, d1_scr, cut_scr, w_scr, acc_scr,
                 *, heads, nkeys, tb, ec, final_norm):
    j = pl.program_id(1)
    ln = V7X_LANES
    ntile = tb // ln
    sub = lax.broadcasted_iota(jnp.int32, (V7X_SUBLANES, ln), 0)
    assert heads == V7X_SUBLANES

    @pl.when(j == 0)
    def _route():
        acc_scr[...] = jnp.zeros_like(acc_scr)

        def tile_body(lt, carry):
            l0 = pl.multiple_of(lt * ln, ln)
            a_pack = [None] * PEER_TOPK
            b_pack = [None] * PEER_TOPK
            for h in range(heads):
                for c in range(2):
                    col = (h * 2 + c) * nkeys
                    qt = pq_ref[pl.ds(l0, ln), col:col + nkeys]
                    s_t = _dot_nt(sk_ref[h * 2 + c], qt) * LOG2E
                    if c == 0:
                        d1_scr[h, :, pl.ds(l0, ln)] = s_t
                    else:
                        s2_scr[h, :, pl.ds(l0, ln)] = s_t
                    top = _top16_of_keys(s_t)
                    pack = a_pack if c == 0 else b_pack
                    for r in range(PEER_TOPK):
                        pack[r] = top[r] if h == 0 else jnp.where(sub == h, top[r], pack[r])
            top_s = _pair_sum_top_set(a_pack, b_pack)
            m = a_pack[0] + b_pack[0]
            zsum = None
            for v in top_s:
                e = jnp.exp2(v - m)
                zsum = e if zsum is None else zsum + e
            shift = m + jnp.log2(zsum)
            d_pack = [a - shift for a in a_pack]
            top_u = _pair_sum_top_set(d_pack, b_pack)
            cut = top_u[0]
            for v in top_u[1:]:
                cut = jnp.minimum(cut, v)
            cut_scr[:, pl.ds(l0, ln)] = cut
            for h in range(heads):
                d1_scr[h, :, pl.ds(l0, ln)] = d1_scr[h, :, pl.ds(l0, ln)] - shift[h:h + 1, :]
            return carry

        lax.fori_loop(0, ntile, tile_body, 0)

    n_i1 = ec // nkeys
    assert n_i1 == V7X_SUBLANES
    i1_base = pl.multiple_of(j * n_i1, n_i1)
    a_t = _dot_nt(u_ref[...], n2_ref[...])
    gl = 0.5 * a_t * (1.0 + lax.erf(a_t * INV_SQRT2))
    for il in range(n_i1):
        for lt in range(ntile):
            lanes = slice(lt * ln, (lt + 1) * ln)
            gate = jnp.zeros((nkeys, ln), F32)
            for h in range(heads):
                d1_rows = d1_scr[h, pl.ds(i1_base, n_i1), lanes]
                u = s2_scr[h, :, lanes] + d1_rows[il:il + 1, :]
                gate = gate + jnp.where(u >= cut_scr[h:h + 1, lanes], jnp.exp2(u), 0.0)
            w_scr[il * nkeys:(il + 1) * nkeys, lanes] = (
                gate * gl[il * nkeys:(il + 1) * nkeys, lanes]).astype(BF16)
    acc_scr[...] += jnp.dot(vt_ref[0], w_scr[...], preferred_element_type=F32)

    @pl.when(j == pl.num_programs(1) - 1)
    def _finish():
        xo = h_ref[...] + jnp.transpose(acc_scr[...])
        y_ref[...] = _rmsnorm_rows(xo, fg_ref[...]) if final_norm else xo


def _peer(n2, pq, h, sk, u_bf, vt_bf, fg, *, heads, nkeys, tb, final_norm):
    t, d = h.shape
    ec = vt_bf.shape[2]
    ne = u_bf.shape[0]
    nq = pq.shape[1]
    nchunk = ne // ec
    est = (2 * tb * d * 2 + 2 * tb * nq * 2 + 2 * tb * d * 4 + 2 * sk.size * 2 + 4 * ec * d * 2
           + 2 * tb * d * 4 + 2 * heads * nkeys * tb * 4 + heads * tb * 4
           + ec * tb * 2 + d * tb * 4 + 3 * ec * tb * 4)
    return pl.pallas_call(
        functools.partial(_peer_kernel, heads=heads, nkeys=nkeys, tb=tb, ec=ec, final_norm=final_norm),
        grid=(t // tb, nchunk),
        in_specs=[
            pl.BlockSpec((tb, d), lambda i, j: (i, 0)),
            pl.BlockSpec((tb, nq), lambda i, j: (i, 0)),
            pl.BlockSpec((tb, d), lambda i, j: (i, 0)),
            pl.BlockSpec(sk.shape, lambda i, j: (0, 0, 0)),
            pl.BlockSpec((ec, d), lambda i, j: (j, 0)),
            pl.BlockSpec((1, d, ec), lambda i, j: (j, 0, 0)),
            pl.BlockSpec((1, d), lambda i, j: (0, 0)),
        ],
        out_specs=pl.BlockSpec((tb, d), lambda i, j: (i, 0)),
        out_shape=jax.ShapeDtypeStruct((t, d), F32),
        scratch_shapes=[
            pltpu.VMEM((heads, nkeys, tb), F32),
            pltpu.VMEM((heads, nkeys, tb), F32),
            pltpu.VMEM((heads, tb), F32),
            pltpu.VMEM((ec, tb), BF16),
            pltpu.VMEM((d, tb), F32),
        ],
        compiler_params=pltpu.CompilerParams(
            dimension_semantics=("parallel", "arbitrary"), vmem_limit_bytes=_vmem_limit(est)),
        name="peer",
    )(n2, pq, h, sk, u_bf, vt_bf, fg.reshape(1, d))


def _layout(d, heads_a, kv_a, hd_a, heads_b, dk, dv, rank):
    widths = [("qa", heads_a * hd_a), ("vb", heads_b * dv), ("rb", heads_b * dv), ("ga", d), ("gb", d),
              ("qb", heads_b * dk), ("kb", heads_b * dk), ("kv", 2 * kv_a * hd_a), ("z", V7X_LANES)]
    lay, off = {}, 0
    for name, w in widths:
        assert off % w == 0
        lay[name] = (off, w)
        off += w
    assert rank <= V7X_LANES
    lay["ncol"] = off
    return lay


def _prep_w_in(w, cols, rank):
    qa, ka, va, qb, kb, vb, rb, zb, ga, gb = jnp.split(w, np.cumsum(cols)[:-1].tolist(), axis=1)
    zpad = jnp.pad(zb, ((0, 0), (0, V7X_LANES - rank)))
    return jnp.concatenate([qa, vb, rb, ga, gb, qb, kb, ka, va, zpad], axis=1).astype(BF16)


def kernel(x_prompt, x_sample, cache_k_win, cache_v_win, state_gla, norm1_g, w_in, w_gla_gate, b_gla_gate,
           attn_sinks, gla_norm_g, w_out, norm2_g, peer_wq, peer_subkeys, peer_u, peer_v, final_norm_g):
    batch, seq, d = x_prompt.shape
    nsamp, dec_seq, _ = x_sample.shape
    assert dec_seq == 1
    depth, _, window, kv_a, hd_a = cache_k_win.shape
    heads_a = attn_sinks.shape[1]
    group = heads_a // kv_a
    _, _, heads_b, dk, dv = state_gla.shape
    rank = w_gla_gate.shape[1]
    peer_heads, _, nkeys, dhalf = peer_subkeys.shape[1:]
    n_exp = peer_u.shape[1]
    assert n_exp == nkeys * nkeys and dhalf == nkeys == V7X_LANES
    cols = (heads_a * hd_a, kv_a * hd_a, kv_a * hd_a, heads_b * dk, heads_b * dk, heads_b * dv,
            heads_b * dv, rank, d, d)
    lay = _layout(d, heads_a, kv_a, hd_a, heads_b, dk, dv, rank)
    kvd = kv_a * hd_a

    yp = x_prompt.reshape(batch * seq, d)
    ys = x_sample.reshape(nsamp, d)
    outs = {k: [] for k in ("kp", "vp", "sp", "ks", "vs", "ss")}
    for l in range(depth):
        w_l = _prep_w_in(w_in[l], cols, rank)
        wg = jnp.pad(w_gla_gate[l], ((0, V7X_LANES - rank), (0, 0))).astype(BF16)
        wo = w_out[l].astype(BF16)
        wq = peer_wq[l].astype(BF16)
        sk = peer_subkeys[l].reshape(peer_heads * 2, nkeys, dhalf).astype(BF16)
        u_bf = peer_u[l].astype(BF16)
        ec = V7X_SUBLANES * nkeys
        vt_bf = jnp.transpose(peer_v[l].reshape(n_exp // ec, ec, d), (0, 2, 1)).astype(BF16)
        last = l == depth - 1

        proj_p, kv_p = _inproj(yp, norm1_g[l], w_l, lay, tm=512)
        oa_p = _swa_prompt(proj_p, kv_p, attn_sinks[l], lay, batch, seq, heads_a, kv_a, hd_a, window)
        og_p, s_p = _gla(proj_p, wg, b_gla_gate[l], gla_norm_g[l], lay, heads_b, dk, dv,
                         batch=batch, seq=seq, chunk=256)
        h_p, n2_p, pq_p = _post(yp, oa_p, og_p, proj_p, wo, norm2_g[l], wq, lay, tm=512)
        yp = _peer(n2_p, pq_p, h_p, sk, u_bf, vt_bf, final_norm_g, heads=peer_heads, nkeys=nkeys,
                   tb=512, final_norm=last)
        kv_last = kv_p.reshape(batch, seq, 2 * kvd)[:, seq - window:, :]
        outs["kp"].append(kv_last[..., :kvd].reshape(batch, window, kv_a, hd_a))
        outs["vp"].append(kv_last[..., kvd:].reshape(batch, window, kv_a, hd_a))
        outs["sp"].append(s_p)

        proj_s, kv_s = _inproj(ys, norm1_g[l], w_l, lay, tm=nsamp)
        qa_s = proj_s[:, :heads_a * hd_a].reshape(nsamp, kv_a, group, hd_a)
        qrep = jnp.transpose(qa_s, (0, 2, 1, 3)).reshape(nsamp, group, 1, kvd)
        qrep = jnp.broadcast_to(qrep, (nsamp, group, kv_a, kvd)).reshape(nsamp, heads_a, kvd)
        row_head = np.array([(r % kv_a) * group + r // kv_a for r in range(heads_a)])
        sink_col = attn_sinks[l][row_head].reshape(heads_a, 1)
        slope_col = jnp.asarray(2.0 ** (-8.0 * (row_head + 1) / heads_a), F32).reshape(heads_a, 1)
        nk_s, nv_s, o_rows = _swa_sample(
            qrep, cache_k_win[l].reshape(nsamp, window, kvd), cache_v_win[l].reshape(nsamp, window, kvd),
            kv_s, sink_col, slope_col, kv_a, hd_a, window)
        oa_s = o_rows[:, ::kv_a, :].reshape(nsamp, group, kv_a, hd_a)
        oa_s = jnp.transpose(oa_s, (0, 2, 1, 3)).reshape(nsamp, d)
        og_s, s_s = _gla(proj_s, wg, b_gla_gate[l], gla_norm_g[l], lay, heads_b, dk, dv,
                         batch=nsamp, seq=1, chunk=V7X_SUBLANES, s0=state_gla[l])
        h_s, n2_s, pq_s = _post(ys, oa_s, og_s, proj_s, wo, norm2_g[l], wq, lay, tm=nsamp)
        ys = _peer(n2_s, pq_s, h_s, sk, u_bf, vt_bf, final_norm_g, heads=peer_heads, nkeys=nkeys,
                   tb=nsamp, final_norm=last)
        outs["ks"].append(nk_s.reshape(nsamp, window, kv_a, hd_a))
        outs["vs"].append(nv_s.reshape(nsamp, window, kv_a, hd_a))
        outs["ss"].append(s_s)

    return (yp.reshape(batch, seq, d), ys.reshape(nsamp, 1, d),
            jnp.stack(outs["kp"]), jnp.stack(outs["vp"]), jnp.stack(outs["sp"]),
            jnp.stack(outs["ks"]), jnp.stack(outs["vs"]), jnp.stack(outs["ss"]))
```

```python
import functools
import math

import numpy as np
import jax
import jax.numpy as jnp
from jax import lax
from jax.experimental import pallas as pl
from jax.experimental.pallas import tpu as pltpu

F32 = jnp.float32
BF16 = jnp.bfloat16

EPS = 1e-6
GATE_TEMP = 16.0
PEER_TOPK = 16
LOG2E = 1.4426950408889634
INV_SQRT2 = 0.7071067811865476

V7X_LANES = 128
V7X_SUBLANES = 8
V7X_MXU_DIM = 256
V7X_VMEM_BYTES = 64 * 1024 * 1024
MIB = 1024 * 1024


def _vmem_limit(estimate_bytes):
    return int(min(estimate_bytes + 16 * MIB, V7X_VMEM_BYTES - 8 * MIB))


def _dot_nt(a, b):
    return lax.dot_general(a, b, (((1,), (1,)), ((), ())), preferred_element_type=F32)


def _sigmoid(z):
    return 1.0 / (1.0 + jnp.exp(-z))


def _rmsnorm_rows(x, g):
    ms = jnp.mean(x * x, axis=-1, keepdims=True)
    return x * lax.rsqrt(ms + EPS) * g


def _inproj_kernel(x_ref, g_ref, w_ref, proj_ref, kv_ref, *, chunks, kv_chunk):
    n = _rmsnorm_rows(x_ref[...], g_ref[...]).astype(BF16)
    for c0, c1 in chunks:
        acc = jnp.dot(n, w_ref[:, c0:c1], preferred_element_type=F32)
        proj_ref[:, c0:c1] = acc.astype(BF16)
        if (c0, c1) == kv_chunk:
            kv_ref[...] = acc


def _inproj(x2d, g, w, lay, tm):
    t, d = x2d.shape
    ncol = w.shape[1]
    kv0, kvw = lay["kv"]
    step = 512
    chunks = tuple((c, min(c + step, ncol)) for c in range(0, ncol, step))
    assert (kv0, kv0 + kvw) in chunks
    est = 2 * tm * d * 4 + 2 * d * ncol * 2 + 2 * tm * ncol * 2 + 2 * tm * kvw * 4
    return pl.pallas_call(
        functools.partial(_inproj_kernel, chunks=chunks, kv_chunk=(kv0, kv0 + kvw)),
        grid=(t // tm,),
        in_specs=[
            pl.BlockSpec((tm, d), lambda i: (i, 0)),
            pl.BlockSpec((1, d), lambda i: (0, 0)),
            pl.BlockSpec((d, ncol), lambda i: (0, 0)),
        ],
        out_specs=[
            pl.BlockSpec((tm, ncol), lambda i: (i, 0)),
            pl.BlockSpec((tm, kvw), lambda i: (i, 0)),
        ],
        out_shape=[
            jax.ShapeDtypeStruct((t, ncol), BF16),
            jax.ShapeDtypeStruct((t, kvw), F32),
        ],
        compiler_params=pltpu.CompilerParams(
            dimension_semantics=("parallel",), vmem_limit_bytes=_vmem_limit(est)),
        name="inproj",
    )(x2d, g.reshape(1, d), w)


def _swa_prompt_kernel(sink_ref, q_ref, kvc_ref, kvp_ref, o_ref, *, heads, kv_heads, hd, window, slopes):
    j = pl.program_id(1)
    group = heads // kv_heads
    kvd = kv_heads * hd
    kvc = kvc_ref[...]
    kvp = kvp_ref[...]
    k_all = jnp.concatenate([kvp[:, :kvd], kvc[:, :kvd]], axis=0).astype(BF16)
    v_all = jnp.concatenate([kvp[:, kvd:], kvc[:, kvd:]], axis=0).astype(BF16)
    t_idx = lax.broadcasted_iota(jnp.int32, (window, 2 * window), 0)
    s_idx = lax.broadcasted_iota(jnp.int32, (window, 2 * window), 1)
    dist = window + t_idx - s_idx
    valid = (dist >= 0) & (dist < window) & ((s_idx >= window) | (j > 0))
    distf = dist.astype(F32)
    scale = hd ** -0.5
    for h in range(heads):
        kvh = h // group
        qh = q_ref[:, h * hd:(h + 1) * hd]
        kh = k_all[:, kvh * hd:(kvh + 1) * hd]
        vh = v_all[:, kvh * hd:(kvh + 1) * hd]
        s = _dot_nt(qh, kh) * scale - slopes[h] * distf
        s = jnp.where(valid, s, -jnp.inf)
        sink = sink_ref[h]
        m = jnp.maximum(jnp.max(s, axis=-1, keepdims=True), sink)
        p = jnp.exp(s - m)
        denom = jnp.sum(p, axis=-1, keepdims=True) + jnp.exp(sink - m)
        oh = jnp.dot((p / denom).astype(BF16), vh, preferred_element_type=F32)
        o_ref[:, h * hd:(h + 1) * hd] = oh.astype(o_ref.dtype)


def _swa_prompt(proj, kv, sinks, lay, batch, seq, heads, kv_heads, hd, window):
    t = batch * seq
    nb = seq // window
    q0, qw = lay["qa"]
    assert q0 == 0
    kvw = kv.shape[1]
    slopes = tuple(float(2.0 ** (-8.0 * (h + 1) / heads)) for h in range(heads))
    est = 2 * window * qw * 2 * 2 + 4 * window * kvw * 4
    return pl.pallas_call(
        functools.partial(_swa_prompt_kernel, heads=heads, kv_heads=kv_heads, hd=hd,
                          window=window, slopes=slopes),
        grid=(batch, nb),
        in_specs=[
            pl.BlockSpec(memory_space=pltpu.SMEM),
            pl.BlockSpec((window, qw), lambda b, j: (b * nb + j, 0)),
            pl.BlockSpec((window, kvw), lambda b, j: (b * nb + j, 0)),
            pl.BlockSpec((window, kvw), lambda b, j: (b * nb + jnp.maximum(j - 1, 0), 0)),
        ],
        out_specs=pl.BlockSpec((window, qw), lambda b, j: (b * nb + j, 0)),
        out_shape=jax.ShapeDtypeStruct((t, qw), BF16),
        compiler_params=pltpu.CompilerParams(
            dimension_semantics=("parallel", "parallel"), vmem_limit_bytes=_vmem_limit(est)),
        name="swa_prompt",
    )(sinks, proj, kv, kv)


def _swa_sample_kernel(qrep_ref, ck_ref, cv_ref, kvn_ref, sink_ref, slope_ref, nk_ref, nv_ref, o_ref,
                       *, bs, kv_heads, hd, window):
    kvd = kv_heads * hd
    rows = qrep_ref.shape[1]
    row_idx = lax.broadcasted_iota(jnp.int32, (window, kvd), 0)
    lane_blk = lax.broadcasted_iota(jnp.int32, (rows, kvd), 1) // hd
    row_kv = lax.broadcasted_iota(jnp.int32, (rows, kvd), 0) % kv_heads
    head_mask = lane_blk == row_kv
    dist = (window - 1 - lax.broadcasted_iota(jnp.int32, (rows, window), 1)).astype(F32)
    sink = sink_ref[...]
    slope = slope_ref[...]
    scale = hd ** -0.5
    for i in range(bs):
        knew = kvn_ref[i:i + 1, :kvd]
        vnew = kvn_ref[i:i + 1, kvd:]
        nk = jnp.where(row_idx == window - 1, knew, pltpu.roll(ck_ref[i], window - 1, axis=0))
        nv = jnp.where(row_idx == window - 1, vnew, pltpu.roll(cv_ref[i], window - 1, axis=0))
        nk_ref[i] = nk
        nv_ref[i] = nv
        qe = jnp.where(head_mask, qrep_ref[i], jnp.zeros_like(qrep_ref[i]))
        s = _dot_nt(qe, nk.astype(BF16)) * scale - slope * dist
        m = jnp.maximum(jnp.max(s, axis=-1, keepdims=True), sink)
        p = jnp.exp(s - m)
        denom = jnp.sum(p, axis=-1, keepdims=True) + jnp.exp(sink - m)
        r = jnp.dot((p / denom).astype(BF16), nv.astype(BF16), preferred_element_type=F32)
        r = jnp.where(head_mask, r, 0.0)
        shift = 1
        while shift < kv_heads:
            r = r + pltpu.roll(r, rows - shift, axis=0)
            shift *= 2
        o_ref[i] = r


def _swa_sample(qrep, ck, cv, kvn, sink_col, slope_col, kv_heads, hd, window, bs=8):
    s, rows, kvd = qrep.shape
    est = 2 * bs * (rows * kvd * 2 + 4 * window * kvd * 4 + 2 * kvd * 4 + rows * kvd * 4)
    return pl.pallas_call(
        functools.partial(_swa_sample_kernel, bs=bs, kv_heads=kv_heads, hd=hd, window=window),
        grid=(s // bs,),
        in_specs=[
            pl.BlockSpec((bs, rows, kvd), lambda i: (i, 0, 0)),
            pl.BlockSpec((bs, window, kvd), lambda i: (i, 0, 0)),
            pl.BlockSpec((bs, window, kvd), lambda i: (i, 0, 0)),
            pl.BlockSpec((bs, 2 * kvd), lambda i: (i, 0)),
            pl.BlockSpec((rows, 1), lambda i: (0, 0)),
            pl.BlockSpec((rows, 1), lambda i: (0, 0)),
        ],
        out_specs=[
            pl.BlockSpec((bs, window, kvd), lambda i: (i, 0, 0)),
            pl.BlockSpec((bs, window, kvd), lambda i: (i, 0, 0)),
            pl.BlockSpec((bs, rows, kvd), lambda i: (i, 0, 0)),
        ],
        out_shape=[
            jax.ShapeDtypeStruct((s, window, kvd), F32),
            jax.ShapeDtypeStruct((s, window, kvd), F32),
            jax.ShapeDtypeStruct((s, rows, kvd), F32),
        ],
        compiler_params=pltpu.CompilerParams(
            dimension_semantics=("parallel",), vmem_limit_bytes=_vmem_limit(est)),
        name="swa_sample",
    )(qrep, ck, cv, kvn, sink_col, slope_col)


def _gla_tables(c):
    nlev = int(round(math.log2(c)))
    assert 1 << nlev == c
    t = np.arange(c)
    u = t[None, :]
    rows = [u <= t[:, None]]
    lvl = np.full((c, c), -1, np.int32)
    lvl[t, t] = 0
    for l in range(1, nlev + 1):
        hs = 1 << (l - 1)
        pos = t % (2 * hs)
        right = pos >= hs
        bnd = (t - pos + hs - 1)[:, None]
        if hs < V7X_SUBLANES:
            rows.append(right[:, None] & (u > bnd) & (u <= t[:, None]))
            rows.append((~right)[:, None] & (u > t[:, None]) & (u <= bnd))
        same = (t[:, None] // (2 * hs)) == (t[None, :] // (2 * hs))
        lvl[same & right[:, None] & (~right)[None, :]] = l
    mall = np.concatenate(rows, axis=0).astype(np.float32)
    return jnp.asarray(mall, dtype=BF16), jnp.asarray(lvl), nlev


def _gla_kernel(*refs, c, nlev, heads, dk, dv, sample_mode):
    if sample_mode:
        (q_ref, k_ref, v_ref, r_ref, z_ref, wg_ref, bg_ref, gn_ref, mall_ref, lvl_ref, s0_ref,
         og_ref, sout_ref) = refs
    else:
        (q_ref, k_ref, v_ref, r_ref, z_ref, wg_ref, bg_ref, gn_ref, mall_ref, lvl_ref,
         og_ref, sout_ref, s_scr) = refs
        step = pl.program_id(1)

        @pl.when(step == 0)
        def _():
            s_scr[...] = jnp.zeros_like(s_scr)

    z = jnp.dot(z_ref[...], wg_ref[...], preferred_element_type=F32) + bg_ref[...]
    la_full = -(jnp.maximum(-z, 0.0) + jnp.log(1.0 + jnp.exp(-jnp.abs(z)))) / GATE_TEMP
    lvl = lvl_ref[...]
    mall = mall_ref[...]
    gn = gn_ref[...]
    row_idx = lax.broadcasted_iota(jnp.int32, (c, 1), 0)
    items = [(s, hh) for s in (range(c) if sample_mode else (None,)) for hh in range(heads)]
    mask_of = {s: (None if s is None else row_idx == s) for s, _ in items}

    def la_of(s, hh):
        la = la_full[:, hh * dk:(hh + 1) * dk]
        return la if s is None else jnp.where(mask_of[s], la, 0.0)

    def split(la):
        hi = la.astype(BF16)
        return jnp.concatenate([hi, (la - hi.astype(F32)).astype(BF16)], axis=1)

    e2s = [jnp.dot(mall, split(la_of(s, hh)), preferred_element_type=F32) for s, hh in items]
    es = [e2[:, :dk] + e2[:, dk:] for e2 in e2s]

    def level_exponents(e, l):
        hs = 1 << (l - 1)
        if hs < V7X_SUBLANES:
            return e[(2 * l - 1) * c:2 * l * c], e[2 * l * c:(2 * l + 1) * c]
        b = e[0:c]
        zero = jnp.zeros((hs, dk), F32)
        eq, ek = [], []
        for r in range(hs - 1, c, 2 * hs):
            eq += [zero, b[r + 1:r + 1 + hs] - b[r:r + 1]]
            ek += [b[r:r + 1] - b[r + 1 - hs:r + 1], zero]
        return jnp.concatenate(eq, axis=0), jnp.concatenate(ek, axis=0)

    qs = [q_ref[:, hh * dk:(hh + 1) * dk].astype(F32) * (dk ** -0.5) for _, hh in items]
    ks, vs = [], []
    for s, hh in items:
        k = k_ref[:, hh * dk:(hh + 1) * dk].astype(F32)
        v = v_ref[:, hh * dv:(hh + 1) * dv]
        if s is not None:
            k = jnp.where(mask_of[s], k, 0.0)
            v = jnp.where(mask_of[s], v, jnp.zeros_like(v))
        ks.append(k)
        vs.append(v)
    scores = [jnp.where(lvl == 0, _dot_nt(q.astype(BF16), k.astype(BF16)), 0.0) for q, k in zip(qs, ks)]
    for l in range(1, nlev + 1):
        lex = [level_exponents(e, l) for e in es]
        scs = [_dot_nt((q * jnp.exp(xq)).astype(BF16), (k * jnp.exp(xk)).astype(BF16))
               for q, k, (xq, xk) in zip(qs, ks, lex)]
        scores = [acc + jnp.where(lvl == l, sc, 0.0) for acc, sc in zip(scores, scs)]
    s_prevs = [s_scr[hh] if s is None else s0_ref[s, hh] for s, hh in items]
    os_ = [jnp.dot(sc.astype(BF16), v, preferred_element_type=F32)
           + jnp.dot((q * jnp.exp(e[0:c])).astype(BF16), sp.astype(BF16), preferred_element_type=F32)
           for sc, v, q, e, sp in zip(scores, vs, qs, es, s_prevs)]
    decs = [jnp.concatenate([jnp.transpose(jnp.broadcast_to(jnp.exp(e[c - 1:c, :]), (dk, dk)))] * (dv // dk),
                            axis=1) for e in es]
    khats = [jnp.transpose(k * jnp.exp(e[c - 1:c] - e[0:c])).astype(BF16) for k, e in zip(ks, es)]
    for (s, hh), dec, sp, kh, v in zip(items, decs, s_prevs, khats, vs):
        s_new = dec * sp + jnp.dot(kh, v, preferred_element_type=F32)
        if s is None:
            s_scr[hh] = s_new
        else:
            sout_ref[s, hh] = s_new
    for (s, hh), o in zip(items, os_):
        rg = r_ref[:, hh * dv:(hh + 1) * dv].astype(F32)
        og = _rmsnorm_rows(o, gn) * (rg * _sigmoid(rg))
        if s is None:
            og_ref[:, hh * dv:(hh + 1) * dv] = og.astype(og_ref.dtype)
        else:
            og_ref[s:s + 1, hh * dv:(hh + 1) * dv] = og[s:s + 1, :].astype(og_ref.dtype)
    if not sample_mode:
        @pl.when(step == pl.num_programs(1) - 1)
        def _():
            sout_ref[0] = s_scr[...]


def _gla(proj, wg, bg, gn, lay, heads, dk, dv, *, batch, seq, chunk, s0=None):
    sample_mode = s0 is not None
    t = proj.shape[0]
    mall, lvl, nlev = _gla_tables(chunk)
    q0, qw = lay["qb"]
    k0, kw = lay["kb"]
    v0, vw = lay["vb"]
    r0_, rw = lay["rb"]
    z0, zw = lay["z"]
    if sample_mode:
        grid = (t // chunk, 1)
        row = lambda b, s: b
        state_block = (chunk, heads, dk, dv)
    else:
        nchunk = seq // chunk
        grid = (batch, nchunk)
        row = lambda b, s: b * nchunk + s
        state_block = (1, heads, dk, dv)
    in_specs = [
        pl.BlockSpec((chunk, qw), lambda b, s: (row(b, s), q0 // qw)),
        pl.BlockSpec((chunk, kw), lambda b, s: (row(b, s), k0 // kw)),
        pl.BlockSpec((chunk, vw), lambda b, s: (row(b, s), v0 // vw)),
        pl.BlockSpec((chunk, rw), lambda b, s: (row(b, s), r0_ // rw)),
        pl.BlockSpec((chunk, zw), lambda b, s: (row(b, s), z0 // zw)),
        pl.BlockSpec(wg.shape, lambda b, s: (0, 0)),
        pl.BlockSpec((1, heads * dk), lambda b, s: (0, 0)),
        pl.BlockSpec((1, dv), lambda b, s: (0, 0)),
        pl.BlockSpec(mall.shape, lambda b, s: (0, 0)),
        pl.BlockSpec(lvl.shape, lambda b, s: (0, 0)),
    ]
    args = [proj, proj, proj, proj, proj, wg, bg.reshape(1, -1), gn.reshape(1, -1), mall, lvl]
    nstate = t if sample_mode else batch
    if sample_mode:
        in_specs.append(pl.BlockSpec(state_block, lambda b, s: (b, 0, 0, 0)))
        args.append(s0)
    og_dtype = F32 if sample_mode else BF16
    est = (2 * chunk * (qw + kw + vw + rw + zw) * 2 + 2 * mall.size * 2 + 2 * lvl.size * 4
           + (4 * chunk if sample_mode else 5) * heads * dk * dv * 4 + 2 * chunk * vw * 4
           + 8 * mall.shape[0] * 2 * dk * 4)
    return pl.pallas_call(
        functools.partial(_gla_kernel, c=chunk, nlev=nlev, heads=heads, dk=dk, dv=dv,
                          sample_mode=sample_mode),
        grid=grid,
        in_specs=in_specs,
        out_specs=[
            pl.BlockSpec((chunk, heads * dv), lambda b, s: (row(b, s), 0)),
            pl.BlockSpec(state_block, lambda b, s: (b, 0, 0, 0)),
        ],
        out_shape=[
            jax.ShapeDtypeStruct((t, heads * dv), og_dtype),
            jax.ShapeDtypeStruct((nstate, heads, dk, dv), F32),
        ],
        scratch_shapes=[] if sample_mode else [pltpu.VMEM((heads, dk, dv), F32)],
        compiler_params=pltpu.CompilerParams(
            dimension_semantics=("arbitrary", "arbitrary"), vmem_limit_bytes=_vmem_limit(est)),
        name="gla_sample" if sample_mode else "gla_prompt",
    )(*args)


def _post_kernel(x_ref, oa_ref, og_ref, ga_ref, gb_ref, wo_ref, g2_ref, wq_ref, h_ref, n2_ref, pq_ref):
    u = (_sigmoid(ga_ref[...].astype(F32)) * oa_ref[...].astype(F32)
         + _sigmoid(gb_ref[...].astype(F32)) * og_ref[...].astype(F32))
    h = x_ref[...] + jnp.dot(u.astype(BF16), wo_ref[...], preferred_element_type=F32)
    h_ref[...] = h
    n2 = _rmsnorm_rows(h, g2_ref[...]).astype(BF16)
    n2_ref[...] = n2
    pq_ref[...] = jnp.dot(n2, wq_ref[...], preferred_element_type=F32).astype(BF16)


def _post(x2d, oa, og, proj, wo, g2, wq, lay, tm):
    t, d = x2d.shape
    ga0, gaw = lay["ga"]
    gb0, gbw = lay["gb"]
    nq = wq.shape[1]
    est = (2 * tm * d * (4 + oa.dtype.itemsize + og.dtype.itemsize + 2 + 2 + 4 + 2)
           + 2 * d * d * 2 + 2 * d * nq * 2 + 2 * tm * nq * 2 + tm * nq * 4)
    return pl.pallas_call(
        _post_kernel,
        grid=(t // tm,),
        in_specs=[
            pl.BlockSpec((tm, d), lambda i: (i, 0)),
            pl.BlockSpec((tm, d), lambda i: (i, 0)),
            pl.BlockSpec((tm, d), lambda i: (i, 0)),
            pl.BlockSpec((tm, gaw), lambda i: (i, ga0 // gaw)),
            pl.BlockSpec((tm, gbw), lambda i: (i, gb0 // gbw)),
            pl.BlockSpec((d, d), lambda i: (0, 0)),
            pl.BlockSpec((1, d), lambda i: (0, 0)),
            pl.BlockSpec((d, nq), lambda i: (0, 0)),
        ],
        out_specs=[
            pl.BlockSpec((tm, d), lambda i: (i, 0)),
            pl.BlockSpec((tm, d), lambda i: (i, 0)),
            pl.BlockSpec((tm, nq), lambda i: (i, 0)),
        ],
        out_shape=[
            jax.ShapeDtypeStruct((t, d), F32),
            jax.ShapeDtypeStruct((t, d), BF16),
            jax.ShapeDtypeStruct((t, nq), BF16),
        ],
        compiler_params=pltpu.CompilerParams(
            dimension_semantics=("parallel",), vmem_limit_bytes=_vmem_limit(est)),
        name="post",
    )(x2d, oa, og, proj, proj, wo, g2.reshape(1, d), wq)


def _oddeven_mergesort_pairs(n):
    pairs = []
    p = 1
    while p < n:
        k = p
        while k >= 1:
            for j in range(k % p, n - k, 2 * k):
                for i in range(min(k, n - j - k)):
                    if (i + j) // (2 * p) == (i + j + k) // (2 * p):
                        pairs.append((i + j, i + j + k))
            k //= 2
        p *= 2
    return pairs


_SORT16_PAIRS = tuple(_oddeven_mergesort_pairs(PEER_TOPK))


def _cmpx(xs, i, j):
    a, b = xs[i], xs[j]
    if b is None:
        return
    if a is None:
        xs[i], xs[j] = b, None
        return
    xs[i], xs[j] = jnp.maximum(a, b), jnp.minimum(a, b)


def _bitonic_sort_desc(xs):
    n = len(xs)
    k = n // 2
    while k >= 1:
        for i in range(n):
            if i & k == 0:
                _cmpx(xs, i, i + k)
        k //= 2
    return xs


def _top_set(x, y):
    n = PEER_TOPK
    x = list(x) + [None] * (n - len(x))
    y = list(y) + [None] * (n - len(y))
    out = []
    for i in range(n):
        a, b = x[i], y[n - 1 - i]
        out.append(b if a is None else (a if b is None else jnp.maximum(a, b)))
    return out


def _merge_desc(x, y):
    return [v for v in _bitonic_sort_desc(_top_set(x, y)) if v is not None]


def _top16_of_keys(s_t):
    nslab = s_t.shape[0] // V7X_SUBLANES
    assert nslab == PEER_TOPK
    xs = [s_t[i * V7X_SUBLANES:(i + 1) * V7X_SUBLANES, :] for i in range(nslab)]
    for i, j in _SORT16_PAIRS:
        _cmpx(xs, i, j)
    shift = V7X_SUBLANES // 2
    while shift >= 1:
        ys = [pltpu.roll(xs[PEER_TOPK - 1 - i], shift, axis=0) for i in range(PEER_TOPK)]
        xs = _bitonic_sort_desc([jnp.maximum(a, b) for a, b in zip(xs, ys)])
        shift //= 2
    return xs


def _pair_sum_top_set(a, b):
    n = PEER_TOPK
    row = lambda i, j0, j1: [a[i] + b[j] for j in range(j0, j1)]
    col = lambda j, i0, i1: [a[i] + b[j] for i in range(i0, i1)]
    top_rows = _merge_desc(_merge_desc(row(0, 0, n), row(1, 0, n // 2)),
                           _merge_desc(row(2, 0, n // 3), row(3, 0, n // 4)))
    rest = _merge_desc(_merge_desc(col(0, 4, n), col(1, 4, n // 2)), col(2, 4, n // 3))
    return _top_set(top_rows, rest)


def _peer_kernel(n2_ref, pq_ref, h_ref, sk_ref, u_ref, vt_ref, fg_ref, y_ref,
                 s2_scr, d1_scr, cut_scr, w_scr, acc_scr,
                 *, heads, nkeys, tb, ec, final_norm):
    j = pl.program_id(1)
    ln = V7X_LANES
    ntile = tb // ln
    sub = lax.broadcasted_iota(jnp.int32, (V7X_SUBLANES, ln), 0)
    assert heads == V7X_SUBLANES

    @pl.when(j == 0)
    def _route():
        acc_scr[...] = jnp.zeros_like(acc_scr)

        def tile_body(lt, carry):
            l0 = pl.multiple_of(lt * ln, ln)
            a_pack = [None] * PEER_TOPK
            b_pack = [None] * PEER_TOPK
            for h in range(heads):
                for c in range(2):
                    col = (h * 2 + c) * nkeys
                    qt = pq_ref[pl.ds(l0, ln), col:col + nkeys]
                    s_t = _dot_nt(sk_ref[h * 2 + c], qt) * LOG2E
                    if c == 0:
                        d1_scr[h, :, pl.ds(l0, ln)] = s_t
                    else:
                        s2_scr[h, :, pl.ds(l0, ln)] = s_t
                    top = _top16_of_keys(s_t)
                    pack = a_pack if c == 0 else b_pack
                    for r in range(PEER_TOPK):
                        pack[r] = top[r] if h == 0 else jnp.where(sub == h, top[r], pack[r])
            top_s = _pair_sum_top_set(a_pack, b_pack)
            m = a_pack[0] + b_pack[0]
            zsum = None
            for v in top_s:
                e = jnp.exp2(v - m)
                zsum = e if zsum is None else zsum + e
            shift = m + jnp.log2(zsum) + 1.0
            d_pack = [a - shift for a in a_pack]
            top_u = _pair_sum_top_set(d_pack, b_pack)
            cut = top_u[0]
            for v in top_u[1:]:
                cut = jnp.minimum(cut, v)
            cut_scr[:, pl.ds(l0, ln)] = cut
            for h in range(heads):
                d1_scr[h, :, pl.ds(l0, ln)] = d1_scr[h, :, pl.ds(l0, ln)] - shift[h:h + 1, :]
            return carry

        lax.fori_loop(0, ntile, tile_body, 0)

    n_i1 = ec // nkeys
    assert n_i1 == V7X_SUBLANES
    i1_base = pl.multiple_of(j * n_i1, n_i1)
    a_t = _dot_nt(u_ref[...], n2_ref[...])
    gl = a_t * (1.0 + lax.erf(a_t * INV_SQRT2))
    for il in range(n_i1):
        for lt in range(ntile):
            lanes = slice(lt * ln, (lt + 1) * ln)
            gate = jnp.zeros((nkeys, ln), F32)
            for h in range(heads):
                d1_rows = d1_scr[h, pl.ds(i1_base, n_i1), lanes]
                u = s2_scr[h, :, lanes] + d1_rows[il:il + 1, :]
                gate = gate + jnp.where(u >= cut_scr[h:h + 1, lanes], jnp.exp2(u), 0.0)
            w_scr[il * nkeys:(il + 1) * nkeys, lanes] = (
                gate * gl[il * nkeys:(il + 1) * nkeys, lanes]).astype(BF16)
    acc_scr[...] += jnp.dot(vt_ref[0], w_scr[...], preferred_element_type=F32)

    @pl.when(j == pl.num_programs(1) - 1)
    def _finish():
        xo = h_ref[...] + jnp.transpose(acc_scr[...])
        y_ref[...] = _rmsnorm_rows(xo, fg_ref[...]) if final_norm else xo


def _peer(n2, pq, h, sk, u_bf, vt_bf, fg, *, heads, nkeys, tb, final_norm):
    t, d = h.shape
    ec = vt_bf.shape[2]
    ne = u_bf.shape[0]
    nq = pq.shape[1]
    nchunk = ne // ec
    est = (2 * tb * d * 2 + 2 * tb * nq * 2 + 2 * tb * d * 4 + 2 * sk.size * 2 + 4 * ec * d * 2
           + 2 * tb * d * 4 + 2 * heads * nkeys * tb * 4 + heads * tb * 4
           + ec * tb * 2 + d * tb * 4 + 3 * ec * tb * 4)
    return pl.pallas_call(
        functools.partial(_peer_kernel, heads=heads, nkeys=nkeys, tb=tb, ec=ec, final_norm=final_norm),
        grid=(t // tb, nchunk),
        in_specs=[
            pl.BlockSpec((tb, d), lambda i, j: (i, 0)),
            pl.BlockSpec((tb, nq), lambda i, j: (i, 0)),
            pl.BlockSpec((tb, d), lambda i, j: (i, 0)),
            pl.BlockSpec(sk.shape, lambda i, j: (0, 0, 0)),
            pl.BlockSpec((ec, d), lambda i, j: (j, 0)),
            pl.BlockSpec((1, d, ec), lambda i, j: (j, 0, 0)),
            pl.BlockSpec((1, d), lambda i, j: (0, 0)),
        ],
        out_specs=pl.BlockSpec((tb, d), lambda i, j: (i, 0)),
        out_shape=jax.ShapeDtypeStruct((t, d), F32),
        scratch_shapes=[
            pltpu.VMEM((heads, nkeys, tb), F32),
            pltpu.VMEM((heads, nkeys, tb), F32),
            pltpu.VMEM((heads, tb), F32),
            pltpu.VMEM((ec, tb), BF16),
            pltpu.VMEM((d, tb), F32),
        ],
        compiler_params=pltpu.CompilerParams(
            dimension_semantics=("parallel", "arbitrary"), vmem_limit_bytes=_vmem_limit(est)),
        name="peer",
    )(n2, pq, h, sk, u_bf, vt_bf, fg.reshape(1, d))


def _layout(d, heads_a, kv_a, hd_a, heads_b, dk, dv, rank):
    widths = [("qa", heads_a * hd_a), ("vb", heads_b * dv), ("rb", heads_b * dv), ("ga", d), ("gb", d),
              ("qb", heads_b * dk), ("kb", heads_b * dk), ("kv", 2 * kv_a * hd_a), ("z", V7X_LANES)]
    lay, off = {}, 0
    for name, w in widths:
        assert off % w == 0
        lay[name] = (off, w)
        off += w
    assert rank <= V7X_LANES
    lay["ncol"] = off
    return lay


def _prep_w_in(w, cols, rank):
    qa, ka, va, qb, kb, vb, rb, zb, ga, gb = jnp.split(w, np.cumsum(cols)[:-1].tolist(), axis=1)
    zpad = jnp.pad(zb, ((0, 0), (0, V7X_LANES - rank)))
    return jnp.concatenate([qa, vb, rb, ga, gb, qb, kb, ka, va, zpad], axis=1).astype(BF16)


def kernel(x_prompt, x_sample, cache_k_win, cache_v_win, state_gla, norm1_g, w_in, w_gla_gate, b_gla_gate,
           attn_sinks, gla_norm_g, w_out, norm2_g, peer_wq, peer_subkeys, peer_u, peer_v, final_norm_g):
    batch, seq, d = x_prompt.shape
    nsamp, dec_seq, _ = x_sample.shape
    assert dec_seq == 1
    depth, _, window, kv_a, hd_a = cache_k_win.shape
    heads_a = attn_sinks.shape[1]
    group = heads_a // kv_a
    _, _, heads_b, dk, dv = state_gla.shape
    rank = w_gla_gate.shape[1]
    peer_heads, _, nkeys, dhalf = peer_subkeys.shape[1:]
    n_exp = peer_u.shape[1]
    assert n_exp == nkeys * nkeys and dhalf == nkeys == V7X_LANES
    cols = (heads_a * hd_a, kv_a * hd_a, kv_a * hd_a, heads_b * dk, heads_b * dk, heads_b * dv,
            heads_b * dv, rank, d, d)
    lay = _layout(d, heads_a, kv_a, hd_a, heads_b, dk, dv, rank)
    kvd = kv_a * hd_a

    yp = x_prompt.reshape(batch * seq, d)
    ys = x_sample.reshape(nsamp, d)
    outs = {k: [] for k in ("kp", "vp", "sp", "ks", "vs", "ss")}
    for l in range(depth):
        w_l = _prep_w_in(w_in[l], cols, rank)
        wg = jnp.pad(w_gla_gate[l], ((0, V7X_LANES - rank), (0, 0))).astype(BF16)
        wo = w_out[l].astype(BF16)
        wq = peer_wq[l].astype(BF16)
        sk = peer_subkeys[l].reshape(peer_heads * 2, nkeys, dhalf).astype(BF16)
        u_bf = peer_u[l].astype(BF16)
        ec = V7X_SUBLANES * nkeys
        vt_bf = jnp.transpose(peer_v[l].reshape(n_exp // ec, ec, d), (0, 2, 1)).astype(BF16)
        last = l == depth - 1

        proj_p, kv_p = _inproj(yp, norm1_g[l], w_l, lay, tm=512)
        oa_p = _swa_prompt(proj_p, kv_p, attn_sinks[l], lay, batch, seq, heads_a, kv_a, hd_a, window)
        og_p, s_p = _gla(proj_p, wg, b_gla_gate[l], gla_norm_g[l], lay, heads_b, dk, dv,
                         batch=batch, seq=seq, chunk=256)
        h_p, n2_p, pq_p = _post(yp, oa_p, og_p, proj_p, wo, norm2_g[l], wq, lay, tm=512)
        yp = _peer(n2_p, pq_p, h_p, sk, u_bf, vt_bf, final_norm_g, heads=peer_heads, nkeys=nkeys,
                   tb=512, final_norm=last)
        kv_last = kv_p.reshape(batch, seq, 2 * kvd)[:, seq - window:, :]
        outs["kp"].append(kv_last[..., :kvd].reshape(batch, window, kv_a, hd_a))
        outs["vp"].append(kv_last[..., kvd:].reshape(batch, window, kv_a, hd_a))
        outs["sp"].append(s_p)

        proj_s, kv_s = _inproj(ys, norm1_g[l], w_l, lay, tm=nsamp)
        qa_s = proj_s[:, :heads_a * hd_a].reshape(nsamp, kv_a, group, hd_a)
        qrep = jnp.transpose(qa_s, (0, 2, 1, 3)).reshape(nsamp, group, 1, kvd)
        qrep = jnp.broadcast_to(qrep, (nsamp, group, kv_a, kvd)).reshape(nsamp, heads_a, kvd)
        row_head = np.array([(r % kv_a) * group + r // kv_a for r in range(heads_a)])
        sink_col = attn_sinks[l][row_head].reshape(heads_a, 1)
        slope_col = jnp.asarray(2.0 ** (-8.0 * (row_head + 1) / heads_a), F32).reshape(heads_a, 1)
        nk_s, nv_s, o_rows = _swa_sample(
            qrep, cache_k_win[l].reshape(nsamp, window, kvd), cache_v_win[l].reshape(nsamp, window, kvd),
            kv_s, sink_col, slope_col, kv_a, hd_a, window)
        oa_s = o_rows[:, ::kv_a, :].reshape(nsamp, group, kv_a, hd_a)
        oa_s = jnp.transpose(oa_s, (0, 2, 1, 3)).reshape(nsamp, d)
        og_s, s_s = _gla(proj_s, wg, b_gla_gate[l], gla_norm_g[l], lay, heads_b, dk, dv,
                         batch=nsamp, seq=1, chunk=V7X_SUBLANES, s0=state_gla[l])
        h_s, n2_s, pq_s = _post(ys, oa_s, og_s, proj_s, wo, norm2_g[l], wq, lay, tm=nsamp)
        ys = _peer(n2_s, pq_s, h_s, sk, u_bf, vt_bf, final_norm_g, heads=peer_heads, nkeys=nkeys,
                   tb=nsamp, final_norm=last)
        outs["ks"].append(nk_s.reshape(nsamp, window, kv_a, hd_a))
        outs["vs"].append(nv_s.reshape(nsamp, window, kv_a, hd_a))
        outs["ss"].append(s_s)

    return (yp.reshape(batch, seq, d), ys.reshape(nsamp, 1, d),
            jnp.stack(outs["kp"]), jnp.stack(outs["vp"]), jnp.stack(outs["sp"]),
            jnp.stack(outs["ks"]), jnp.stack(outs["vs"]), jnp.stack(outs["ss"]))
```
